```python
import math
import jax, jax.numpy as jnp
from jax import lax
import numpy as np

D_MODEL = 2048
BATCH = 1
SEQ = 8192
DEPTH = 1

CHUNK = 64
Q_BLOCK = 128
CONV_WIDTH = D_MODEL // 2
ATTN_WIDTH = D_MODEL - CONV_WIDTH
CONV_KERNEL = 31
N_HEADS = 8
V_HEAD_DIM = ATTN_WIDTH // N_HEADS
QK_HEAD_DIM = V_HEAD_DIM // 2
D_FF = 5632
N_MOD = 9
IN_WIDTH = 2 * CONV_WIDTH + 3 * ATTN_WIDTH
EPS = 1e-6
NEG_INF = -1e30

kernel_name = 'hybrid_conformer_diffattn_macaron_block'


def rms_norm(x, g):
    xf = x.astype(jnp.float32)
    y = xf * lax.rsqrt(jnp.mean(xf * xf, axis=-1, keepdims=True) + EPS)
    return (y * g.astype(jnp.float32)).astype(x.dtype)


def layer_norm(x, g, b):
    xf = x.astype(jnp.float32)
    mu = jnp.mean(xf, axis=-1, keepdims=True)
    xc = xf - mu
    y = xc * lax.rsqrt(jnp.mean(xc * xc, axis=-1, keepdims=True) + EPS)
    return (y * g.astype(jnp.float32) + b.astype(jnp.float32)).astype(x.dtype)


def modulate(h, shift, scale):
    return h * (1 + scale) + shift


def swiglu(h, w_gate, w_up, w_down):
    return (jax.nn.silu(h @ w_gate) * (h @ w_up)) @ w_down


def alibi_slopes():
    return 2.0 ** (-8.0 * (jnp.arange(N_HEADS, dtype=jnp.float32) + 1.0) / N_HEADS)


def lambda_init_fn(layer_idx):
    return 0.8 - 0.6 * math.exp(-0.3 * layer_idx)


def conformer_conv(u, b_in, w_dw, b_dw, ln_g, ln_b):
    u = u + b_in
    a, g = jnp.split(u, 2, axis=-1)
    v = a * jax.nn.sigmoid(g)
    v = lax.conv_general_dilated(
        v, w_dw[:, None, :], window_strides=(1,),
        padding=[(CONV_KERNEL - 1, 0)],
        dimension_numbers=('NWC', 'WIO', 'NWC'),
        feature_group_count=CONV_WIDTH) + b_dw
    return jax.nn.silu(layer_norm(v, ln_g, ln_b))


def diff_attention(q, k, v, lam, subln_g, lam_init):
    B, S = q.shape[0], q.shape[1]
    n_blk = S // Q_BLOCK
    scale = QK_HEAD_DIM ** -0.5
    slopes = alibi_slopes()
    k_pos = jnp.arange(S)
    k_chunk = k_pos // CHUNK
    q_blocks = q.reshape(B, n_blk, Q_BLOCK, N_HEADS, 2, QK_HEAD_DIM).transpose(1, 0, 2, 3, 4, 5)

    def block(args):
        q_blk, i = args
        q_pos = i * Q_BLOCK + jnp.arange(Q_BLOCK)
        s = jnp.einsum('bqhmd,bkhmd->bhmqk', q_blk, k).astype(jnp.float32) * scale
        dist = jnp.abs(q_pos[:, None] - k_pos[None, :]).astype(jnp.float32)
        allowed = k_chunk[None, :] <= (q_pos // CHUNK)[:, None]
        s = jnp.where(allowed, s - slopes[None, :, None, None, None] * dist, NEG_INF)
        p = jax.nn.softmax(s, axis=-1)
        a = p[:, :, 0] - lam * p[:, :, 1]
        return jnp.einsum('bhqk,bkhe->bqhe', a.astype(v.dtype), v)

    o = lax.map(block, (q_blocks, jnp.arange(n_blk)))
    o = o.transpose(1, 0, 2, 3, 4).reshape(B, S, N_HEADS, V_HEAD_DIM)
    o = rms_norm(o, subln_g) * (1.0 - lam_init)
    return o.reshape(B, S, ATTN_WIDTH)


def setup_inputs(seed: int = 0) -> dict:
    key = jax.random.key(seed)
    ks = jax.random.split(key, 26)
    f32 = jnp.float32
    D, L = D_MODEL, DEPTH

    def nrm(k, shape, std):
        return jax.random.normal(k, shape, f32) * std

    return {
        'x': nrm(ks[0], (BATCH, SEQ, D), 1.0),
        'c': nrm(ks[1], (BATCH, D), 1.0),
        'w_ada': nrm(ks[2], (L, D, N_MOD * D), 0.5 * D ** -0.5),
        'b_ada': nrm(ks[3], (L, N_MOD * D), 0.01),
        'g_pre': 1.0 + nrm(ks[4], (L, 3, D), 0.02),
        'g_post': 1.0 + nrm(ks[5], (L, 3, D), 0.02),
        'w_ffn1_gate': nrm(ks[6], (L, D, D_FF), D ** -0.5),
        'w_ffn1_up': nrm(ks[7], (L, D, D_FF), D ** -0.5),
        'w_ffn1_down': nrm(ks[8], (L, D_FF, D), D_FF ** -0.5),
        'w_in': nrm(ks[9], (L, D, IN_WIDTH), D ** -0.5),
        'b_in_conv': nrm(ks[10], (L, 2 * CONV_WIDTH), 0.02),
        'w_dw': nrm(ks[11], (L, CONV_KERNEL, CONV_WIDTH), CONV_KERNEL ** -0.5),
        'b_dw': nrm(ks[12], (L, CONV_WIDTH), 0.02),
        'conv_ln_g': 1.0 + nrm(ks[13], (L, CONV_WIDTH), 0.02),
        'conv_ln_b': nrm(ks[14], (L, CONV_WIDTH), 0.02),
        'lam_q1': nrm(ks[15], (L, QK_HEAD_DIM), 0.1),
        'lam_k1': nrm(ks[16], (L, QK_HEAD_DIM), 0.1),
        'lam_q2': nrm(ks[17], (L, QK_HEAD_DIM), 0.1),
        'lam_k2': nrm(ks[18], (L, QK_HEAD_DIM), 0.1),
        'subln_g': 1.0 + nrm(ks[19], (L, V_HEAD_DIM), 0.02),
        'w_out': nrm(ks[20], (L, CONV_WIDTH + ATTN_WIDTH, D), (CONV_WIDTH + ATTN_WIDTH) ** -0.5),
        'w_ffn2_gate': nrm(ks[21], (L, D, D_FF), D ** -0.5),
        'w_ffn2_up': nrm(ks[22], (L, D, D_FF), D ** -0.5),
        'w_ffn2_down': nrm(ks[23], (L, D_FF, D), D_FF ** -0.5),
    }


def reference(x, c, w_ada, b_ada, g_pre, g_post, w_ffn1_gate, w_ffn1_up, w_ffn1_down,
              w_in, b_in_conv, w_dw, b_dw, conv_ln_g, conv_ln_b,
              lam_q1, lam_k1, lam_q2, lam_k2, subln_g, w_out,
              w_ffn2_gate, w_ffn2_up, w_ffn2_down):
    B, S = x.shape[0], x.shape[1]
    split_pts = [2 * CONV_WIDTH, 2 * CONV_WIDTH + ATTN_WIDTH, 2 * CONV_WIDTH + 2 * ATTN_WIDTH]
    for l in range(DEPTH):
        lam_init = lambda_init_fn(l)
        mod = (jax.nn.silu(c) @ w_ada[l] + b_ada[l])[:, None, :]
        sh1, sc1, gt1, sh2, sc2, gt2, sh3, sc3, gt3 = jnp.split(mod, N_MOD, axis=-1)

        h = modulate(rms_norm(x, g_pre[l, 0]), sh1, sc1)
        x = x + 0.5 * gt1 * rms_norm(swiglu(h, w_ffn1_gate[l], w_ffn1_up[l], w_ffn1_down[l]), g_post[l, 0])

        h = modulate(rms_norm(x, g_pre[l, 1]), sh2, sc2)
        proj = h @ w_in[l]
        u_conv, q, k, v = jnp.split(proj, split_pts, axis=-1)
        y_conv = conformer_conv(u_conv, b_in_conv[l], w_dw[l], b_dw[l], conv_ln_g[l], conv_ln_b[l])
        q = q.reshape(B, S, N_HEADS, 2, QK_HEAD_DIM)
        k = k.reshape(B, S, N_HEADS, 2, QK_HEAD_DIM)
        v = v.reshape(B, S, N_HEADS, V_HEAD_DIM)
        lam = (jnp.exp(jnp.sum(lam_q1[l].astype(jnp.float32) * lam_k1[l].astype(jnp.float32)))
               - jnp.exp(jnp.sum(lam_q2[l].astype(jnp.float32) * lam_k2[l].astype(jnp.float32)))
               + lam_init)
        y_attn = diff_attention(q, k, v, lam, subln_g[l], lam_init)
        y_mix = jnp.concatenate([y_conv, y_attn], axis=-1) @ w_out[l]
        x = x + gt2 * rms_norm(y_mix, g_post[l, 1])

        h = modulate(rms_norm(x, g_pre[l, 2]), sh3, sc3)
        x = x + 0.5 * gt3 * rms_norm(swiglu(h, w_ffn2_gate[l], w_ffn2_up[l], w_ffn2_down[l]), g_post[l, 2])
    return x
```

```python
import functools
import math

import jax
import jax.numpy as jnp
from jax import lax
from jax.experimental import pallas as pl
from jax.experimental.pallas import tpu as pltpu

D_MODEL = 2048
SEQ = 8192
CHUNK = 64
CONV_WIDTH = D_MODEL // 2
ATTN_WIDTH = D_MODEL - CONV_WIDTH
CONV_KERNEL = 31
N_HEADS = 8
V_HEAD_DIM = ATTN_WIDTH // N_HEADS
QK_HEAD_DIM = V_HEAD_DIM // 2
D_FF = 5632
N_MOD = 9
IN_WIDTH = 2 * CONV_WIDTH + 3 * ATTN_WIDTH
EPS = 1e-6
NEG_INF = -1e30
LAMBDA_INIT = 0.8 - 0.6 * math.exp(-0.3 * 0)

F32 = jnp.float32
BF16 = jnp.bfloat16

V7X_VMEM_BUDGET_BYTES = 56 * 1024 * 1024

ADA_TN = 1024
FFN_TM = 512
FFN_TF = 512
PROJ_TM = 512
PROJ_TN = 1024
CONV_TM = 256
CONV_HALO = 32
CONV_RC = 64
ATT_TQ = 512
ATT_TK = 512


def _params(semantics, vmem_bytes):
    return pltpu.CompilerParams(
        dimension_semantics=semantics,
        vmem_limit_bytes=min(int(vmem_bytes), V7X_VMEM_BUDGET_BYTES),
    )


def _rms(x, g):
    ms = jnp.mean(x * x, axis=-1, keepdims=True)
    return x * lax.rsqrt(ms + EPS) * g


def _ada_kernel(c_ref, w_ref, b_ref, o_ref):
    c = c_ref[...]
    s = (c * jax.nn.sigmoid(c)).astype(BF16)
    s8 = jnp.broadcast_to(s, (8, D_MODEL))
    r = jnp.dot(s8, w_ref[...].astype(BF16), preferred_element_type=F32)
    o_ref[...] = r[0:1, :] + b_ref[...]


def _ada(c, w, b):
    n = w.shape[1]
    return pl.pallas_call(
        _ada_kernel,
        grid=(n // ADA_TN,),
        in_specs=[
            pl.BlockSpec((1, D_MODEL), lambda j: (0, 0)),
            pl.BlockSpec((D_MODEL, ADA_TN), lambda j: (0, j)),
            pl.BlockSpec((1, ADA_TN), lambda j: (0, j)),
        ],
        out_specs=pl.BlockSpec((1, ADA_TN), lambda j: (0, j)),
        out_shape=jax.ShapeDtypeStruct((1, n), F32),
        compiler_params=_params(("parallel",), 2 * D_MODEL * ADA_TN * 4 + D_MODEL * ADA_TN * 2 + (8 << 20)),
        name="adaln",
    )(c, w, b.reshape(1, n))


def _ffn_kernel(x_ref, shift_ref, scale_ref, gate_ref, gpre_ref, gpost_ref,
                wg_ref, wu_ref, wd_ref, o_ref, h_ref, acc_ref):
    j = pl.program_id(1)

    @pl.when(j == 0)
    def _():
        y = _rms(x_ref[...], gpre_ref[...])
        h_ref[...] = (y * (1.0 + scale_ref[...]) + shift_ref[...]).astype(BF16)
        acc_ref[...] = jnp.zeros_like(acc_ref)

    h = h_ref[...]
    g = jnp.dot(h, wg_ref[...], preferred_element_type=F32)
    u = jnp.dot(h, wu_ref[...], preferred_element_type=F32)
    a = (g * jax.nn.sigmoid(g) * u).astype(BF16)
    acc_ref[...] += jnp.dot(a, wd_ref[...], preferred_element_type=F32)

    @pl.when(j == pl.num_programs(1) - 1)
    def _():
        r = _rms(acc_ref[...], gpost_ref[...])
        o_ref[...] = x_ref[...] + 0.5 * gate_ref[...] * r


def _ffn(x, mod, mod_base, g_pre, g_post, wg, wu, wd):
    tm, tf = FFN_TM, FFN_TF
    row = lambda k: pl.BlockSpec((1, D_MODEL), lambda i, j, k=k: (0, k))
    vec = pl.BlockSpec((1, D_MODEL), lambda i, j: (0, 0))
    vmem = (2 * 2 * tm * D_MODEL * 4
            + tm * D_MODEL * (2 + 4)
            + 2 * 3 * D_MODEL * tf * 2
            + 4 * tm * tf * 4
            + (6 << 20))
    return pl.pallas_call(
        _ffn_kernel,
        grid=(SEQ // tm, D_FF // tf),
        in_specs=[
            pl.BlockSpec((tm, D_MODEL), lambda i, j: (i, 0)),
            row(mod_base), row(mod_base + 1), row(mod_base + 2),
            vec, vec,
            pl.BlockSpec((D_MODEL, tf), lambda i, j: (0, j)),
            pl.BlockSpec((D_MODEL, tf), lambda i, j: (0, j)),
            pl.BlockSpec((tf, D_MODEL), lambda i, j: (j, 0)),
        ],
        out_specs=pl.BlockSpec((tm, D_MODEL), lambda i, j: (i, 0)),
        out_shape=jax.ShapeDtypeStruct((SEQ, D_MODEL), F32),
        scratch_shapes=[pltpu.VMEM((tm, D_MODEL), BF16), pltpu.VMEM((tm, D_MODEL), F32)],
        compiler_params=_params(("parallel", "arbitrary"), vmem),
        name="ffn",
    )(x, mod, mod, mod, g_pre.reshape(1, D_MODEL), g_post.reshape(1, D_MODEL), wg, wu, wd)


N_U_BLOCKS = 2 * CONV_WIDTH // PROJ_TN


def _inproj_kernel(x_ref, shift_ref, scale_ref, gpre_ref, w_ref, u_ref, qkv_ref, h_ref):
    j = pl.program_id(1)

    @pl.when(j == 0)
    def _():
        y = _rms(x_ref[...], gpre_ref[...])
        h_ref[...] = (y * (1.0 + scale_ref[...]) + shift_ref[...]).astype(BF16)

    p = jnp.dot(h_ref[...], w_ref[...], preferred_element_type=F32)

    @pl.when(j < N_U_BLOCKS)
    def _():
        u_ref[...] = p

    @pl.when(j >= N_U_BLOCKS)
    def _():
        qkv_ref[...] = p.astype(BF16)


def _inproj(x, mod, g_pre, w_in):
    tm, tn = PROJ_TM, PROJ_TN
    vmem = (2 * tm * D_MODEL * 4 + tm * D_MODEL * 2 + 2 * D_MODEL * tn * 2
            + 2 * tm * tn * (4 + 2) + 2 * tm * tn * 4 + (6 << 20))
    return pl.pallas_call(
        _inproj_kernel,
        grid=(SEQ // tm, IN_WIDTH // tn),
        in_specs=[
            pl.BlockSpec((tm, D_MODEL), lambda i, j: (i, 0)),
            pl.BlockSpec((1, D_MODEL), lambda i, j: (0, 3)),
            pl.BlockSpec((1, D_MODEL), lambda i, j: (0, 4)),
            pl.BlockSpec((1, D_MODEL), lambda i, j: (0, 0)),
            pl.BlockSpec((D_MODEL, tn), lambda i, j: (0, j)),
        ],
        out_specs=[
            pl.BlockSpec((tm, tn), lambda i, j: (i, jnp.minimum(j, N_U_BLOCKS - 1))),
            pl.BlockSpec((tm, tn), lambda i, j: (i, jnp.maximum(j - N_U_BLOCKS, 0))),
        ],
        out_shape=[
            jax.ShapeDtypeStruct((SEQ, 2 * CONV_WIDTH), F32),
            jax.ShapeDtypeStruct((SEQ, 3 * ATTN_WIDTH), BF16),
        ],
        scratch_shapes=[pltpu.VMEM((tm, D_MODEL), BF16)],
        compiler_params=_params(("parallel", "arbitrary"), vmem),
        name="inproj",
    )(x, mod, mod, g_pre.reshape(1, D_MODEL), w_in)


def _conv_kernel(u_ref, uprev_ref, bin_ref, wdw_ref, bdw_ref, lng_ref, lnb_ref, o_ref,
                 vbuf_ref, conv_ref):
    i = pl.program_id(0)
    b = bin_ref[...]

    def glu(u):
        u = u + b
        return u[:, :CONV_WIDTH] * jax.nn.sigmoid(u[:, CONV_WIDTH:])

    vbuf_ref[0:CONV_HALO, :] = jnp.where(i > 0, glu(uprev_ref[...]), 0.0)
    vbuf_ref[CONV_HALO:, :] = glu(u_ref[...])

    lead = CONV_HALO - (CONV_KERNEL - 1)
    for cb in range(CONV_WIDTH // 128):
        cs = slice(cb * 128, (cb + 1) * 128)
        for r in range(CONV_TM // CONV_RC):
            acc = jnp.zeros((CONV_RC, 128), F32)
            for j in range(CONV_KERNEL):
                lo = r * CONV_RC + lead + j
                acc = acc + vbuf_ref[lo:lo + CONV_RC, cs] * wdw_ref[j:j + 1, cs]
            conv_ref[r * CONV_RC:(r + 1) * CONV_RC, cs] = acc

    v = conv_ref[...] + bdw_ref[...]
    mu = jnp.mean(v, axis=-1, keepdims=True)
    xc = v - mu
    var = jnp.mean(xc * xc, axis=-1, keepdims=True)
    y = xc * lax.rsqrt(var + EPS) * lng_ref[...] + lnb_ref[...]
    o_ref[...] = (y * jax.nn.sigmoid(y)).astype(BF16)


def _conv(u, b_in, w_dw, b_dw, ln_g, ln_b):
    tm = CONV_TM
    halo_blocks = tm // CONV_HALO
    vec = lambda n: pl.BlockSpec((1, n), lambda i: (0, 0))
    vmem = (2 * tm * 2 * CONV_WIDTH * 4 + 2 * CONV_HALO * 2 * CONV_WIDTH * 4
            + (2 * tm + CONV_HALO) * CONV_WIDTH * 4 + 2 * tm * CONV_WIDTH * 2
            + 6 * tm * CONV_WIDTH * 4 + (6 << 20))
    return pl.pallas_call(
        _conv_kernel,
        grid=(SEQ // tm,),
        in_specs=[
            pl.BlockSpec((tm, 2 * CONV_WIDTH), lambda i: (i, 0)),
            pl.BlockSpec((CONV_HALO, 2 * CONV_WIDTH), lambda i: (jnp.maximum(i * halo_blocks - 1, 0), 0)),
            vec(2 * CONV_WIDTH),
            pl.BlockSpec((CONV_KERNEL, CONV_WIDTH), lambda i: (0, 0)),
            vec(CONV_WIDTH), vec(CONV_WIDTH), vec(CONV_WIDTH),
        ],
        out_specs=pl.BlockSpec((tm, CONV_WIDTH), lambda i: (i, 0)),
        out_shape=jax.ShapeDtypeStruct((SEQ, CONV_WIDTH), BF16),
        scratch_shapes=[pltpu.VMEM((tm + CONV_HALO, CONV_WIDTH), F32), pltpu.VMEM((tm, CONV_WIDTH), F32)],
        compiler_params=_params(("parallel",), vmem),
        name="convbranch",
    )(u, u, b_in.reshape(1, -1), w_dw, b_dw.reshape(1, -1), ln_g.reshape(1, -1), ln_b.reshape(1, -1))


def _attn_kernel(slope_ref, lam_ref, q_ref, k_ref, v_ref, g_ref, o_ref,
                 qs_ref, bias_ref, m_ref, l_ref, acc_ref):
    tq, tk = ATT_TQ, ATT_TK
    h = pl.program_id(0)
    i = pl.program_id(1)
    slope = slope_ref[h]

    q = q_ref[...] * jnp.asarray(QK_HEAD_DIM ** -0.5, BF16)
    lane = lax.broadcasted_iota(jnp.int32, (tq, V_HEAD_DIM), 1)
    zero = jnp.zeros_like(q)
    qs_ref[0:tq, :] = jnp.where(lane < QK_HEAD_DIM, q, zero)
    qs_ref[tq:, :] = jnp.where(lane >= QK_HEAD_DIM, q, zero)

    row = lax.broadcasted_iota(jnp.int32, (tq, tk), 0)
    col = lax.broadcasted_iota(jnp.int32, (tq, tk), 1)
    rel = (col - row).astype(F32)

    m_ref[...] = jnp.full_like(m_ref, NEG_INF)
    l_ref[...] = jnp.zeros_like(l_ref)
    acc_ref[...] = jnp.zeros_like(acc_ref)

    def update(j, s):
        v = v_ref[pl.ds(pl.multiple_of(j * tk, tk), tk), :]
        m_prev = m_ref[...]
        m_new = jnp.maximum(m_prev, jnp.max(s, axis=-1, keepdims=True))
        alpha = jnp.exp(m_prev - m_new)
        p = jnp.exp(s - m_new)
        l_ref[...] = alpha * l_ref[...] + jnp.sum(p, axis=-1, keepdims=True)
        acc_ref[...] = alpha * acc_ref[...] + jnp.dot(p.astype(BF16), v, preferred_element_type=F32)
        m_ref[...] = m_new

    def scores(j):
        k = k_ref[pl.ds(pl.multiple_of(j * tk, tk), tk), :]
        return lax.dot_general(qs_ref[...], k, (((1,), (1,)), ((), ())), preferred_element_type=F32)

    bias_ref[0:tq, :] = slope * rel
    bias_ref[tq:, :] = slope * rel

    def body(j, carry):
        tile_dist = ((i - j) * tq).astype(F32)
        update(j, scores(j) + (bias_ref[...] - slope * tile_dist))
        return carry

    lax.fori_loop(0, i, body, 0)

    allowed = (col // CHUNK) <= (row // CHUNK)
    bias_d = jnp.where(allowed, -slope * jnp.abs(rel), NEG_INF)
    bias_ref[0:tq, :] = bias_d
    bias_ref[tq:, :] = bias_d
    bias2 = bias_ref[...]
    update(i, jnp.where(bias2 > 0.5 * NEG_INF, scores(i) + bias2, NEG_INF))

    o = acc_ref[...] / l_ref[...]
    d = o[0:tq, :] - lam_ref[0] * o[tq:, :]
    o_ref[...] = (_rms(d, g_ref[...]) * (1.0 - LAMBDA_INIT)).astype(BF16)


def _attention(qkv, slopes, lam, subln_g):
    tq, tk = ATT_TQ, ATT_TK
    nh = N_HEADS
    smem = pl.BlockSpec(memory_space=pltpu.SMEM)
    vmem = (2 * 2 * SEQ * V_HEAD_DIM * 2 + 4 * tq * V_HEAD_DIM * 2
            + 2 * tq * V_HEAD_DIM * 2 + 2 * tq * tk * 4 + 2 * 2 * tq * 128 * 4 + 2 * tq * V_HEAD_DIM * 4
            + 8 * 2 * tq * tk * 4 + (6 << 20))
    return pl.pallas_call(
        _attn_kernel,
        grid=(nh, SEQ // tq),
        in_specs=[
            smem, smem,
            pl.BlockSpec((tq, V_HEAD_DIM), lambda h, i: (i, h)),
            pl.BlockSpec((SEQ, V_HEAD_DIM), lambda h, i: (0, nh + h)),
            pl.BlockSpec((SEQ, V_HEAD_DIM), lambda h, i: (0, 2 * nh + h)),
            pl.BlockSpec((1, V_HEAD_DIM), lambda h, i: (0, 0)),
        ],
        out_specs=pl.BlockSpec((tq, V_HEAD_DIM), lambda h, i: (i, h)),
        out_shape=jax.ShapeDtypeStruct((SEQ, ATTN_WIDTH), BF16),
        scratch_shapes=[
            pltpu.VMEM((2 * tq, V_HEAD_DIM), BF16),
            pltpu.VMEM((2 * tq, tk), F32),
            pltpu.VMEM((2 * tq, 1), F32),
            pltpu.VMEM((2 * tq, 1), F32),
            pltpu.VMEM((2 * tq, V_HEAD_DIM), F32),
        ],
        compiler_params=_params(("parallel", "parallel"), vmem),
        name="diffattn",
    )(slopes, lam, qkv, qkv, qkv, subln_g.reshape(1, V_HEAD_DIM))


def _outproj_kernel(x_ref, gate_ref, gpost_ref, yc_ref, ya_ref, wc_ref, wa_ref, o_ref):
    y = jnp.dot(yc_ref[...], wc_ref[...], preferred_element_type=F32)
    y = y + jnp.dot(ya_ref[...], wa_ref[...], preferred_element_type=F32)
    o_ref[...] = x_ref[...] + gate_ref[...] * _rms(y, gpost_ref[...])


def _outproj(x, mod, g_post, y_conv, y_attn, w_out):
    tm = PROJ_TM
    vmem = (2 * 2 * tm * D_MODEL * 4 + 2 * 2 * tm * CONV_WIDTH * 2 + 2 * D_MODEL * D_MODEL * 2
            + 3 * tm * D_MODEL * 4 + (6 << 20))
    return pl.pallas_call(
        _outproj_kernel,
        grid=(SEQ // tm,),
        in_specs=[
            pl.BlockSpec((tm, D_MODEL), lambda i: (i, 0)),
            pl.BlockSpec((1, D_MODEL), lambda i: (0, 5)),
            pl.BlockSpec((1, D_MODEL), lambda i: (0, 0)),
            pl.BlockSpec((tm, CONV_WIDTH), lambda i: (i, 0)),
            pl.BlockSpec((tm, ATTN_WIDTH), lambda i: (i, 0)),
            pl.BlockSpec((CONV_WIDTH, D_MODEL), lambda i: (0, 0)),
            pl.BlockSpec((ATTN_WIDTH, D_MODEL), lambda i: (1, 0)),
        ],
        out_specs=pl.BlockSpec((tm, D_MODEL), lambda i: (i, 0)),
        out_shape=jax.ShapeDtypeStruct((SEQ, D_MODEL), F32),
        compiler_params=_params(("parallel",), vmem),
        name="outproj",
    )(x, mod, g_post.reshape(1, D_MODEL), y_conv, y_attn, w_out, w_out)


def kernel(x, c, w_ada, b_ada, g_pre, g_post, w_ffn1_gate, w_ffn1_up, w_ffn1_down, w_in, b_in_conv, w_dw, b_dw, conv_ln_g, conv_ln_b, lam_q1, lam_k1, lam_q2, lam_k2, subln_g, w_out, w_ffn2_gate, w_ffn2_up, w_ffn2_down):
    bf = lambda w: w[0].astype(BF16)
    x2 = x.reshape(SEQ, D_MODEL)

    mod = _ada(c, w_ada[0], b_ada[0])

    x2 = _ffn(x2, mod, 0, g_pre[0, 0], g_post[0, 0], bf(w_ffn1_gate), bf(w_ffn1_up), bf(w_ffn1_down))

    u_conv, qkv = _inproj(x2, mod, g_pre[0, 1], bf(w_in))
    y_conv = _conv(u_conv, b_in_conv[0], w_dw[0], b_dw[0], conv_ln_g[0], conv_ln_b[0])

    lam = (jnp.exp(jnp.sum(lam_q1[0].astype(F32) * lam_k1[0].astype(F32)))
           - jnp.exp(jnp.sum(lam_q2[0].astype(F32) * lam_k2[0].astype(F32)))
           + LAMBDA_INIT).reshape(1)
    slopes = 2.0 ** (-8.0 * (jnp.arange(N_HEADS, dtype=F32) + 1.0) / N_HEADS)
    y_attn = _attention(qkv, slopes, lam, subln_g[0])

    x2 = _outproj(x2, mod, g_post[0, 1], y_conv, y_attn, bf(w_out))

    x2 = _ffn(x2, mod, 6, g_pre[0, 2], g_post[0, 2], bf(w_ffn2_gate), bf(w_ffn2_up), bf(w_ffn2_down))
    return x2.reshape(1, SEQ, D_MODEL)
```

```python
import functools
import math

import jax
import jax.numpy as jnp
from jax import lax
from jax.experimental import pallas as pl
from jax.experimental.pallas import tpu as pltpu

D_MODEL = 2048
SEQ = 8192
CHUNK = 64
CONV_WIDTH = D_MODEL // 2
ATTN_WIDTH = D_MODEL - CONV_WIDTH
CONV_KERNEL = 31
N_HEADS = 8
V_HEAD_DIM = ATTN_WIDTH // N_HEADS
QK_HEAD_DIM = V_HEAD_DIM // 2
D_FF = 5632
N_MOD = 9
IN_WIDTH = 2 * CONV_WIDTH + 3 * ATTN_WIDTH
EPS = 1e-6
NEG_INF = -1e30
LAMBDA_INIT = 0.8 - 0.6 * math.exp(-0.3 * 0)

F32 = jnp.float32
BF16 = jnp.bfloat16

V7X_VMEM_BUDGET_BYTES = 56 * 1024 * 1024

ADA_TN = 1024
FFN_TM = 512
FFN_TF = 512
PROJ_TM = 512
PROJ_TN = 1024
CONV_TM = 256
CONV_HALO = 32
CONV_RC = 64
ATT_TQ = 512
ATT_TK = 512
ATT_CB = 256


def _params(semantics, vmem_bytes, flags=None):
    return pltpu.CompilerParams(
        dimension_semantics=semantics,
        vmem_limit_bytes=min(int(vmem_bytes), V7X_VMEM_BUDGET_BYTES),
        flags=flags,
    )


def _rms(x, g):
    ms = jnp.mean(x * x, axis=-1, keepdims=True)
    return x * lax.rsqrt(ms + EPS) * g


def _ada_kernel(c_ref, w_ref, b_ref, o_ref):
    c = c_ref[...]
    s = (c * jax.nn.sigmoid(c)).astype(BF16)
    s8 = jnp.broadcast_to(s, (8, D_MODEL))
    r = jnp.dot(s8, w_ref[...].astype(BF16), preferred_element_type=F32)
    o_ref[...] = r[0:1, :] + b_ref[...]


def _ada(c, w, b):
    n = w.shape[1]
    return pl.pallas_call(
        _ada_kernel,
        grid=(n // ADA_TN,),
        in_specs=[
            pl.BlockSpec((1, D_MODEL), lambda j: (0, 0)),
            pl.BlockSpec((D_MODEL, ADA_TN), lambda j: (0, j)),
            pl.BlockSpec((1, ADA_TN), lambda j: (0, j)),
        ],
        out_specs=pl.BlockSpec((1, ADA_TN), lambda j: (0, j)),
        out_shape=jax.ShapeDtypeStruct((1, n), F32),
        compiler_params=_params(("parallel",), 2 * D_MODEL * ADA_TN * 4 + D_MODEL * ADA_TN * 2 + (8 << 20)),
        name="adaln",
    )(c, w, b.reshape(1, n))


def _ffn_kernel(x_ref, shift_ref, scale_ref, gate_ref, gpre_ref, gpost_ref,
                wg_ref, wu_ref, wd_ref, o_ref, h_ref, acc_ref):
    j = pl.program_id(1)

    @pl.when(j == 0)
    def _():
        y = _rms(x_ref[...], gpre_ref[...])
        h_ref[...] = (y * (1.0 + scale_ref[...]) + shift_ref[...]).astype(BF16)
        acc_ref[...] = jnp.zeros_like(acc_ref)

    h = h_ref[...]
    g = jnp.dot(h, wg_ref[...], preferred_element_type=F32)
    u = jnp.dot(h, wu_ref[...], preferred_element_type=F32)
    a = (g * jax.nn.sigmoid(g) * u).astype(BF16)
    acc_ref[...] += jnp.dot(a, wd_ref[...], preferred_element_type=F32)

    @pl.when(j == pl.num_programs(1) - 1)
    def _():
        r = _rms(acc_ref[...], gpost_ref[...])
        o_ref[...] = x_ref[...] + 0.5 * gate_ref[...] * r


def _ffn(x, mod, mod_base, g_pre, g_post, wg, wu, wd):
    tm, tf = FFN_TM, FFN_TF
    row = lambda k: pl.BlockSpec((1, D_MODEL), lambda i, j, k=k: (0, k))
    vec = pl.BlockSpec((1, D_MODEL), lambda i, j: (0, 0))
    vmem = (2 * 2 * tm * D_MODEL * 4
            + tm * D_MODEL * (2 + 4)
            + 2 * 3 * D_MODEL * tf * 2
            + 4 * tm * tf * 4
            + (6 << 20))
    return pl.pallas_call(
        _ffn_kernel,
        grid=(SEQ // tm, D_FF // tf),
        in_specs=[
            pl.BlockSpec((tm, D_MODEL), lambda i, j: (i, 0)),
            row(mod_base), row(mod_base + 1), row(mod_base + 2),
            vec, vec,
            pl.BlockSpec((D_MODEL, tf), lambda i, j: (0, j)),
            pl.BlockSpec((D_MODEL, tf), lambda i, j: (0, j)),
            pl.BlockSpec((tf, D_MODEL), lambda i, j: (j, 0)),
        ],
        out_specs=pl.BlockSpec((tm, D_MODEL), lambda i, j: (i, 0)),
        out_shape=jax.ShapeDtypeStruct((SEQ, D_MODEL), F32),
        scratch_shapes=[pltpu.VMEM((tm, D_MODEL), BF16), pltpu.VMEM((tm, D_MODEL), F32)],
        compiler_params=_params(("parallel", "arbitrary"), vmem),
        name="ffn",
    )(x, mod, mod, mod, g_pre.reshape(1, D_MODEL), g_post.reshape(1, D_MODEL), wg, wu, wd)


N_U_BLOCKS = 2 * CONV_WIDTH // PROJ_TN
N_QK_BLOCKS = 2 * ATTN_WIDTH // PROJ_TN


def _inproj_kernel(x_ref, shift_ref, scale_ref, gpre_ref, w_ref, wvt_ref, u_ref, qk_ref, vt_ref, h_ref):
    j = pl.program_id(1)

    @pl.when(j == 0)
    def _():
        y = _rms(x_ref[...], gpre_ref[...])
        h_ref[...] = (y * (1.0 + scale_ref[...]) + shift_ref[...]).astype(BF16)

    @pl.when(j < N_U_BLOCKS)
    def _():
        u_ref[...] = jnp.dot(h_ref[...], w_ref[...], preferred_element_type=F32)

    @pl.when(jnp.logical_and(j >= N_U_BLOCKS, j < N_U_BLOCKS + N_QK_BLOCKS))
    def _():
        qk_ref[...] = jnp.dot(h_ref[...], w_ref[...], preferred_element_type=F32).astype(BF16)

    @pl.when(j == N_U_BLOCKS + N_QK_BLOCKS)
    def _():
        vt = lax.dot_general(wvt_ref[...], h_ref[...], (((1,), (1,)), ((), ())),
                             preferred_element_type=F32)
        vt_ref[...] = vt.astype(BF16)


def _inproj(x, mod, g_pre, w_uqk, w_vt):
    tm, tn = PROJ_TM, PROJ_TN
    n_main = N_U_BLOCKS + N_QK_BLOCKS
    vmem = (2 * tm * D_MODEL * 4 + tm * D_MODEL * 2 + 2 * D_MODEL * tn * 2 + 2 * ATTN_WIDTH * D_MODEL * 2
            + 2 * tm * tn * (4 + 2) + 2 * ATTN_WIDTH * tm * 2 + 2 * tm * tn * 4 + (6 << 20))
    return pl.pallas_call(
        _inproj_kernel,
        grid=(SEQ // tm, n_main + 1),
        in_specs=[
            pl.BlockSpec((tm, D_MODEL), lambda i, j: (i, 0)),
            pl.BlockSpec((1, D_MODEL), lambda i, j: (0, 3)),
            pl.BlockSpec((1, D_MODEL), lambda i, j: (0, 4)),
            pl.BlockSpec((1, D_MODEL), lambda i, j: (0, 0)),
            pl.BlockSpec((D_MODEL, tn), lambda i, j: (0, jnp.minimum(j, n_main - 1))),
            pl.BlockSpec((ATTN_WIDTH, D_MODEL), lambda i, j: (0, 0)),
        ],
        out_specs=[
            pl.BlockSpec((tm, tn), lambda i, j: (i, jnp.minimum(j, N_U_BLOCKS - 1))),
            pl.BlockSpec((tm, tn), lambda i, j: (i, jnp.clip(j - N_U_BLOCKS, 0, N_QK_BLOCKS - 1))),
            pl.BlockSpec((ATTN_WIDTH, tm), lambda i, j: (0, i)),
        ],
        out_shape=[
            jax.ShapeDtypeStruct((SEQ, 2 * CONV_WIDTH), F32),
            jax.ShapeDtypeStruct((SEQ, 2 * ATTN_WIDTH), BF16),
            jax.ShapeDtypeStruct((ATTN_WIDTH, SEQ), BF16),
        ],
        scratch_shapes=[pltpu.VMEM((tm, D_MODEL), BF16)],
        compiler_params=_params(("parallel", "arbitrary"), vmem),
        name="inproj",
    )(x, mod, mod, g_pre.reshape(1, D_MODEL), w_uqk, w_vt)


def _conv_kernel(u_ref, uprev_ref, bin_ref, wdw_ref, bdw_ref, lng_ref, lnb_ref, o_ref,
                 vbuf_ref, conv_ref):
    i = pl.program_id(0)
    b = bin_ref[...]

    def glu(u):
        u = u + b
        return u[:, :CONV_WIDTH] * jax.nn.sigmoid(u[:, CONV_WIDTH:])

    vbuf_ref[0:CONV_HALO, :] = jnp.where(i > 0, glu(uprev_ref[...]), 0.0)
    vbuf_ref[CONV_HALO:, :] = glu(u_ref[...])

    lead = CONV_HALO - (CONV_KERNEL - 1)
    for cb in range(CONV_WIDTH // 128):
        cs = slice(cb * 128, (cb + 1) * 128)
        for r in range(CONV_TM // CONV_RC):
            acc = jnp.zeros((CONV_RC, 128), F32)
            for j in range(CONV_KERNEL):
                lo = r * CONV_RC + lead + j
                acc = acc + vbuf_ref[lo:lo + CONV_RC, cs] * wdw_ref[j:j + 1, cs]
            conv_ref[r * CONV_RC:(r + 1) * CONV_RC, cs] = acc

    v = conv_ref[...] + bdw_ref[...]
    mu = jnp.mean(v, axis=-1, keepdims=True)
    xc = v - mu
    var = jnp.mean(xc * xc, axis=-1, keepdims=True)
    y = xc * lax.rsqrt(var + EPS) * lng_ref[...] + lnb_ref[...]
    o_ref[...] = (y * jax.nn.sigmoid(y)).astype(BF16)


def _conv(u, b_in, w_dw, b_dw, ln_g, ln_b):
    tm = CONV_TM
    halo_blocks = tm // CONV_HALO
    vec = lambda n: pl.BlockSpec((1, n), lambda i: (0, 0))
    vmem = (2 * tm * 2 * CONV_WIDTH * 4 + 2 * CONV_HALO * 2 * CONV_WIDTH * 4
            + (2 * tm + CONV_HALO) * CONV_WIDTH * 4 + 2 * tm * CONV_WIDTH * 2
            + 6 * tm * CONV_WIDTH * 4 + (6 << 20))
    return pl.pallas_call(
        _conv_kernel,
        grid=(SEQ // tm,),
        in_specs=[
            pl.BlockSpec((tm, 2 * CONV_WIDTH), lambda i: (i, 0)),
            pl.BlockSpec((CONV_HALO, 2 * CONV_WIDTH), lambda i: (jnp.maximum(i * halo_blocks - 1, 0), 0)),
            vec(2 * CONV_WIDTH),
            pl.BlockSpec((CONV_KERNEL, CONV_WIDTH), lambda i: (0, 0)),
            vec(CONV_WIDTH), vec(CONV_WIDTH), vec(CONV_WIDTH),
        ],
        out_specs=pl.BlockSpec((tm, CONV_WIDTH), lambda i: (i, 0)),
        out_shape=jax.ShapeDtypeStruct((SEQ, CONV_WIDTH), BF16),
        scratch_shapes=[pltpu.VMEM((tm + CONV_HALO, CONV_WIDTH), F32), pltpu.VMEM((tm, CONV_WIDTH), F32)],
        compiler_params=_params(("parallel",), vmem),
        name="convbranch",
    )(u, u, b_in.reshape(1, -1), w_dw, b_dw.reshape(1, -1), ln_g.reshape(1, -1), ln_b.reshape(1, -1))


def _attn_kernel(slope_ref, lam_ref, q_ref, k_ref, vt_ref, g_ref, o_ref,
                 qs_ref, bias_ref, s_ref, p_ref, alpha_ref, m_ref, l_ref, acc_ref):
    tq, tk, cb = ATT_TQ, ATT_TK, ATT_CB
    nb = 2 * tq // cb
    nqc = tq // cb
    h = pl.program_id(0)
    i = pl.program_id(1)
    slope = slope_ref[h]

    q = q_ref[...] * jnp.asarray(QK_HEAD_DIM ** -0.5, BF16)
    lane = lax.broadcasted_iota(jnp.int32, (tq, V_HEAD_DIM), 1)
    zero = jnp.zeros_like(q)
    qs_ref[0:tq, :] = jnp.where(lane < QK_HEAD_DIM, q, zero)
    qs_ref[tq:, :] = jnp.where(lane >= QK_HEAD_DIM, q, zero)

    for c in range(nqc):
        krel = lax.broadcasted_iota(jnp.int32, (tk, cb), 0)
        qrel = lax.broadcasted_iota(jnp.int32, (tk, cb), 1) + c * cb
        rel = (krel - qrel).astype(F32)
        bias_ref[0, c] = slope * rel
        allowed = (krel // CHUNK) <= (qrel // CHUNK)
        bias_ref[1, c] = jnp.where(allowed, -slope * jnp.abs(rel), NEG_INF)

    m_ref[...] = jnp.full_like(m_ref, NEG_INF)
    l_ref[...] = jnp.zeros_like(l_ref)
    acc_ref[...] = jnp.zeros_like(acc_ref)

    def scores(t, b):
        k = k_ref[pl.ds(pl.multiple_of(t * tk, tk), tk), :]
        s_ref[b % 2] = lax.dot_general(k, qs_ref[b * cb:(b + 1) * cb, :], (((1,), (1,)), ((), ())),
                                       preferred_element_type=F32)

    def softmax(t, b):
        off = slope * ((i - t) * tq).astype(F32)
        diag = (t == i).astype(jnp.int32)
        s = s_ref[b % 2] + bias_ref[diag, b % nqc]
        m_prev = m_ref[b]
        m_new = jnp.maximum(m_prev, jnp.max(s, axis=0, keepdims=True) - off)
        alpha = jnp.exp(m_prev - m_new)
        p = jnp.exp(s - (m_new + off))
        l_ref[b] = alpha * l_ref[b] + jnp.sum(p, axis=0, keepdims=True)
        m_ref[b] = m_new
        alpha_ref[b % 2] = alpha
        p_ref[b % 2] = p.astype(BF16)

    def values(t, b):
        vt = vt_ref[:, pl.ds(pl.multiple_of(t * tk, tk), tk)]
        pv = jnp.dot(vt, p_ref[b % 2], preferred_element_type=F32)
        acc_ref[b] = alpha_ref[b % 2] * acc_ref[b] + pv

    scores(0, 0)
    scores(0, 1)
    softmax(0, 0)

    def body(j, carry):
        for b in range(nb):
            scores(j + (b + 2) // nb, (b + 2) % nb)
            softmax(j + (b + 1) // nb, (b + 1) % nb)
            values(j, b)
        return carry

    lax.fori_loop(0, i, body, 0)

    for b in range(nb):
        if b + 2 < nb:
            scores(i, b + 2)
        if b + 1 < nb:
            softmax(i, b + 1)
        values(i, b)

    lam = lam_ref[0]
    for c in range(nqc):
        d = acc_ref[c] / l_ref[c] - lam * (acc_ref[nqc + c] / l_ref[nqc + c])
        ms = jnp.mean(d * d, axis=0, keepdims=True)
        y = d * lax.rsqrt(ms + EPS) * g_ref[...] * (1.0 - LAMBDA_INIT)
        o_ref[c * cb:(c + 1) * cb, :] = y.T.astype(BF16)


def _attention(qk, vt, slopes, lam, subln_g):
    tq, tk, cb = ATT_TQ, ATT_TK, ATT_CB
    nh = N_HEADS
    nb = 2 * tq // cb
    smem = pl.BlockSpec(memory_space=pltpu.SMEM)
    scratch = [
        pltpu.VMEM((2 * tq, V_HEAD_DIM), BF16),
        pltpu.VMEM((2, tq // cb, tk, cb), F32),
        pltpu.VMEM((2, tk, cb), F32),
        pltpu.VMEM((2, tk, cb), BF16),
        pltpu.VMEM((2, 1, cb), F32),
        pltpu.VMEM((nb, 1, cb), F32),
        pltpu.VMEM((nb, 1, cb), F32),
        pltpu.VMEM((nb, V_HEAD_DIM, cb), F32),
    ]
    vmem = (2 * 2 * SEQ * V_HEAD_DIM * 2 + 4 * tq * V_HEAD_DIM * 2
            + 2 * tq * V_HEAD_DIM * 2 + 2 * tq * tk * 4 + 2 * tk * cb * 6 + (2 + 2 * nb) * 8 * cb * 4
            + nb * V_HEAD_DIM * cb * 4 + 8 * tk * cb * 4 + (6 << 20))
    return pl.pallas_call(
        _attn_kernel,
        grid=(nh, SEQ // tq),
        in_specs=[
            smem, smem,
            pl.BlockSpec((tq, V_HEAD_DIM), lambda h, i: (i, h)),
            pl.BlockSpec((SEQ, V_HEAD_DIM), lambda h, i: (0, nh + h)),
            pl.BlockSpec((V_HEAD_DIM, SEQ), lambda h, i: (h, 0)),
            pl.BlockSpec((V_HEAD_DIM, 1), lambda h, i: (0, 0)),
        ],
        out_specs=pl.BlockSpec((tq, V_HEAD_DIM), lambda h, i: (i, h)),
        out_shape=jax.ShapeDtypeStruct((SEQ, ATTN_WIDTH), BF16),
        scratch_shapes=scratch,
        compiler_params=_params(("parallel", "parallel"), vmem),
        name="diffattn",
    )(slopes, lam, qk, qk, vt, subln_g.reshape(V_HEAD_DIM, 1))


def _outproj_kernel(x_ref, gate_ref, gpost_ref, yc_ref, ya_ref, wc_ref, wa_ref, o_ref):
    y = jnp.dot(yc_ref[...], wc_ref[...], preferred_element_type=F32)
    y = y + jnp.dot(ya_ref[...], wa_ref[...], preferred_element_type=F32)
    o_ref[...] = x_ref[...] + gate_ref[...] * _rms(y, gpost_ref[...])


def _outproj(x, mod, g_post, y_conv, y_attn, w_out):
    tm = PROJ_TM
    vmem = (2 * 2 * tm * D_MODEL * 4 + 2 * 2 * tm * CONV_WIDTH * 2 + 2 * D_MODEL * D_MODEL * 2
            + 3 * tm * D_MODEL * 4 + (6 << 20))
    return pl.pallas_call(
        _outproj_kernel,
        grid=(SEQ // tm,),
        in_specs=[
            pl.BlockSpec((tm, D_MODEL), lambda i: (i, 0)),
            pl.BlockSpec((1, D_MODEL), lambda i: (0, 5)),
            pl.BlockSpec((1, D_MODEL), lambda i: (0, 0)),
            pl.BlockSpec((tm, CONV_WIDTH), lambda i: (i, 0)),
            pl.BlockSpec((tm, ATTN_WIDTH), lambda i: (i, 0)),
            pl.BlockSpec((CONV_WIDTH, D_MODEL), lambda i: (0, 0)),
            pl.BlockSpec((ATTN_WIDTH, D_MODEL), lambda i: (1, 0)),
        ],
        out_specs=pl.BlockSpec((tm, D_MODEL), lambda i: (i, 0)),
        out_shape=jax.ShapeDtypeStruct((SEQ, D_MODEL), F32),
        compiler_params=_params(("parallel",), vmem),
        name="outproj",
    )(x, mod, g_post.reshape(1, D_MODEL), y_conv, y_attn, w_out, w_out)


def kernel(x, c, w_ada, b_ada, g_pre, g_post, w_ffn1_gate, w_ffn1_up, w_ffn1_down, w_in, b_in_conv, w_dw, b_dw, conv_ln_g, conv_ln_b, lam_q1, lam_k1, lam_q2, lam_k2, subln_g, w_out, w_ffn2_gate, w_ffn2_up, w_ffn2_down):
    bf = lambda w: w[0].astype(BF16)
    x2 = x.reshape(SEQ, D_MODEL)

    mod = _ada(c, w_ada[0], b_ada[0])

    x2 = _ffn(x2, mod, 0, g_pre[0, 0], g_post[0, 0], bf(w_ffn1_gate), bf(w_ffn1_up), bf(w_ffn1_down))

    w_in0 = w_in[0]
    n_uqk = 2 * CONV_WIDTH + 2 * ATTN_WIDTH
    u_conv, qk, vt = _inproj(x2, mod, g_pre[0, 1], w_in0[:, :n_uqk].astype(BF16),
                             w_in0[:, n_uqk:].T.astype(BF16))
    y_conv = _conv(u_conv, b_in_conv[0], w_dw[0], b_dw[0], conv_ln_g[0], conv_ln_b[0])

    lam = (jnp.exp(jnp.sum(lam_q1[0].astype(F32) * lam_k1[0].astype(F32)))
           - jnp.exp(jnp.sum(lam_q2[0].astype(F32) * lam_k2[0].astype(F32)))
           + LAMBDA_INIT).reshape(1)
    slopes = 2.0 ** (-8.0 * (jnp.arange(N_HEADS, dtype=F32) + 1.0) / N_HEADS)
    y_attn = _attention(qk, vt, slopes, lam, subln_g[0])

    x2 = _outproj(x2, mod, g_post[0, 1], y_conv, y_attn, bf(w_out))

    x2 = _ffn(x2, mod, 6, g_pre[0, 2], g_post[0, 2], bf(w_ffn2_gate), bf(w_ffn2_up), bf(w_ffn2_down))
    return x2.reshape(1, SEQ, D_MODEL)
```

```python
import functools
import math

import jax
import jax.numpy as jnp
import numpy as np
from jax import lax
from jax.experimental import pallas as pl
from jax.experimental.pallas import tpu as pltpu

D_MODEL = 2048
SEQ = 8192
CHUNK = 64
CONV_WIDTH = D_MODEL // 2
ATTN_WIDTH = D_MODEL - CONV_WIDTH
CONV_KERNEL = 31
N_HEADS = 8
V_HEAD_DIM = ATTN_WIDTH // N_HEADS
QK_HEAD_DIM = V_HEAD_DIM // 2
D_FF = 5632
N_MOD = 9
IN_WIDTH = 2 * CONV_WIDTH + 3 * ATTN_WIDTH
EPS = 1e-6
NEG_INF = -1e30
LAMBDA_INIT = 0.8 - 0.6 * math.exp(-0.3 * 0)

F32 = jnp.float32
BF16 = jnp.bfloat16

V7X_VMEM_BUDGET_BYTES = 56 * 1024 * 1024

ADA_TN = 1024
FFN_TM = 512
FFN_TF = 512
PROJ_TM = 512
PROJ_TN = 1024
CONV_TM = 256
CONV_HALO = 32
CONV_RC = 64
ATT_TQ = 1024
ATT_TK = 512
ATT_CB = 256
ATT_SKEW_S = 4
ATT_SKEW_P = 2
ATT_RING = 4


def _params(semantics, vmem_bytes, flags=None):
    return pltpu.CompilerParams(
        dimension_semantics=semantics,
        vmem_limit_bytes=min(int(vmem_bytes), V7X_VMEM_BUDGET_BYTES),
        flags=flags,
    )


def _rms(x, g):
    ms = jnp.mean(x * x, axis=-1, keepdims=True)
    return x * lax.rsqrt(ms + EPS) * g


def _ada_kernel(c_ref, w_ref, b_ref, o_ref):
    c = c_ref[...]
    s = (c * jax.nn.sigmoid(c)).astype(BF16)
    s8 = jnp.broadcast_to(s, (8, D_MODEL))
    r = jnp.dot(s8, w_ref[...].astype(BF16), preferred_element_type=F32)
    o_ref[...] = r[0:1, :] + b_ref[...]


def _ada(c, w, b):
    n = w.shape[1]
    return pl.pallas_call(
        _ada_kernel,
        grid=(n // ADA_TN,),
        in_specs=[
            pl.BlockSpec((1, D_MODEL), lambda j: (0, 0)),
            pl.BlockSpec((D_MODEL, ADA_TN), lambda j: (0, j)),
            pl.BlockSpec((1, ADA_TN), lambda j: (0, j)),
        ],
        out_specs=pl.BlockSpec((1, ADA_TN), lambda j: (0, j)),
        out_shape=jax.ShapeDtypeStruct((1, n), F32),
        compiler_params=_params(("parallel",), 2 * D_MODEL * ADA_TN * 4 + D_MODEL * ADA_TN * 2 + (8 << 20)),
        name="adaln",
    )(c, w, b.reshape(1, n))


def _ffn_kernel(x_ref, shift_ref, scale_ref, gate_ref, gpre_ref, gpost_ref,
                wg_ref, wu_ref, wd_ref, o_ref, h_ref, acc_ref):
    j = pl.program_id(1)

    @pl.when(j == 0)
    def _():
        y = _rms(x_ref[...], gpre_ref[...])
        h_ref[...] = (y * (1.0 + scale_ref[...]) + shift_ref[...]).astype(BF16)
        acc_ref[...] = jnp.zeros_like(acc_ref)

    h = h_ref[...]
    g = jnp.dot(h, wg_ref[...], preferred_element_type=F32)
    u = jnp.dot(h, wu_ref[...], preferred_element_type=F32)
    a = (g * jax.nn.sigmoid(g) * u).astype(BF16)
    acc_ref[...] += jnp.dot(a, wd_ref[...], preferred_element_type=F32)

    @pl.when(j == pl.num_programs(1) - 1)
    def _():
        r = _rms(acc_ref[...], gpost_ref[...])
        o_ref[...] = x_ref[...] + 0.5 * gate_ref[...] * r


def _ffn(x, mod, mod_base, g_pre, g_post, wg, wu, wd):
    tm, tf = FFN_TM, FFN_TF
    row = lambda k: pl.BlockSpec((1, D_MODEL), lambda i, j, k=k: (0, k))
    vec = pl.BlockSpec((1, D_MODEL), lambda i, j: (0, 0))
    vmem = (2 * 2 * tm * D_MODEL * 4
            + tm * D_MODEL * (2 + 4)
            + 2 * 3 * D_MODEL * tf * 2
            + 4 * tm * tf * 4
            + (6 << 20))
    return pl.pallas_call(
        _ffn_kernel,
        grid=(SEQ // tm, D_FF // tf),
        in_specs=[
            pl.BlockSpec((tm, D_MODEL), lambda i, j: (i, 0)),
            row(mod_base), row(mod_base + 1), row(mod_base + 2),
            vec, vec,
            pl.BlockSpec((D_MODEL, tf), lambda i, j: (0, j)),
            pl.BlockSpec((D_MODEL, tf), lambda i, j: (0, j)),
            pl.BlockSpec((tf, D_MODEL), lambda i, j: (j, 0)),
        ],
        out_specs=pl.BlockSpec((tm, D_MODEL), lambda i, j: (i, 0)),
        out_shape=jax.ShapeDtypeStruct((SEQ, D_MODEL), F32),
        scratch_shapes=[pltpu.VMEM((tm, D_MODEL), BF16), pltpu.VMEM((tm, D_MODEL), F32)],
        compiler_params=_params(("parallel", "arbitrary"), vmem),
        name="ffn",
    )(x, mod, mod, mod, g_pre.reshape(1, D_MODEL), g_post.reshape(1, D_MODEL), wg, wu, wd)


N_U_BLOCKS = 2 * CONV_WIDTH // PROJ_TN
N_QK_BLOCKS = 2 * ATTN_WIDTH // PROJ_TN
assert PROJ_TN == ATTN_WIDTH
LOG2E = math.log2(math.e)
Q_SCALE_LOG2 = QK_HEAD_DIM ** -0.5 * LOG2E


def _inproj_kernel(x_ref, shift_ref, scale_ref, gpre_ref, w_ref, wvt_ref, u_ref, qk_ref, vt_ref, h_ref):
    j = pl.program_id(1)

    @pl.when(j == 0)
    def _():
        y = _rms(x_ref[...], gpre_ref[...])
        h_ref[...] = (y * (1.0 + scale_ref[...]) + shift_ref[...]).astype(BF16)

    @pl.when(j < N_U_BLOCKS)
    def _():
        u_ref[...] = jnp.dot(h_ref[...], w_ref[...], preferred_element_type=F32)

    @pl.when(jnp.logical_and(j >= N_U_BLOCKS, j < N_U_BLOCKS + N_QK_BLOCKS))
    def _():
        c = jnp.where(j == N_U_BLOCKS, Q_SCALE_LOG2, 1.0)
        qk_ref[...] = (jnp.dot(h_ref[...], w_ref[...], preferred_element_type=F32) * c).astype(BF16)

    @pl.when(j == N_U_BLOCKS + N_QK_BLOCKS)
    def _():
        vt = lax.dot_general(wvt_ref[...], h_ref[...], (((1,), (1,)), ((), ())),
                             preferred_element_type=F32)
        vt_ref[...] = vt.astype(BF16)


def _inproj(x, mod, g_pre, w_uqk, w_vt):
    tm, tn = PROJ_TM, PROJ_TN
    n_main = N_U_BLOCKS + N_QK_BLOCKS
    vmem = (2 * tm * D_MODEL * 4 + tm * D_MODEL * 2 + 2 * D_MODEL * tn * 2 + 2 * ATTN_WIDTH * D_MODEL * 2
            + 2 * tm * tn * (4 + 2) + 2 * ATTN_WIDTH * tm * 2 + 2 * tm * tn * 4 + (6 << 20))
    return pl.pallas_call(
        _inproj_kernel,
        grid=(SEQ // tm, n_main + 1),
        in_specs=[
            pl.BlockSpec((tm, D_MODEL), lambda i, j: (i, 0)),
            pl.BlockSpec((1, D_MODEL), lambda i, j: (0, 3)),
            pl.BlockSpec((1, D_MODEL), lambda i, j: (0, 4)),
            pl.BlockSpec((1, D_MODEL), lambda i, j: (0, 0)),
            pl.BlockSpec((D_MODEL, tn), lambda i, j: (0, jnp.minimum(j, n_main - 1))),
            pl.BlockSpec((ATTN_WIDTH, D_MODEL), lambda i, j: (0, 0)),
        ],
        out_specs=[
            pl.BlockSpec((tm, tn), lambda i, j: (i, jnp.minimum(j, N_U_BLOCKS - 1))),
            pl.BlockSpec((tm, tn), lambda i, j: (i, jnp.clip(j - N_U_BLOCKS, 0, N_QK_BLOCKS - 1))),
            pl.BlockSpec((ATTN_WIDTH, tm), lambda i, j: (0, i)),
        ],
        out_shape=[
            jax.ShapeDtypeStruct((SEQ, 2 * CONV_WIDTH), F32),
            jax.ShapeDtypeStruct((SEQ, 2 * ATTN_WIDTH), BF16),
            jax.ShapeDtypeStruct((ATTN_WIDTH, SEQ), BF16),
        ],
        scratch_shapes=[pltpu.VMEM((tm, D_MODEL), BF16)],
        compiler_params=_params(("parallel", "arbitrary"), vmem),
        name="inproj",
    )(x, mod, mod, g_pre.reshape(1, D_MODEL), w_uqk, w_vt)


def _conv_kernel(u_ref, uprev_ref, bin_ref, wdw_ref, bdw_ref, lng_ref, lnb_ref, o_ref,
                 vbuf_ref, conv_ref):
    i = pl.program_id(0)
    b = bin_ref[...]

    def glu(u):
        u = u + b
        return u[:, :CONV_WIDTH] * jax.nn.sigmoid(u[:, CONV_WIDTH:])

    vbuf_ref[0:CONV_HALO, :] = jnp.where(i > 0, glu(uprev_ref[...]), 0.0)
    vbuf_ref[CONV_HALO:, :] = glu(u_ref[...])

    lead = CONV_HALO - (CONV_KERNEL - 1)
    for cb in range(CONV_WIDTH // 128):
        cs = slice(cb * 128, (cb + 1) * 128)
        for r in range(CONV_TM // CONV_RC):
            acc = jnp.zeros((CONV_RC, 128), F32)
            for j in range(CONV_KERNEL):
                lo = r * CONV_RC + lead + j
                acc = acc + vbuf_ref[lo:lo + CONV_RC, cs] * wdw_ref[j:j + 1, cs]
            conv_ref[r * CONV_RC:(r + 1) * CONV_RC, cs] = acc

    v = conv_ref[...] + bdw_ref[...]
    mu = jnp.mean(v, axis=-1, keepdims=True)
    xc = v - mu
    var = jnp.mean(xc * xc, axis=-1, keepdims=True)
    y = xc * lax.rsqrt(var + EPS) * lng_ref[...] + lnb_ref[...]
    o_ref[...] = (y * jax.nn.sigmoid(y)).astype(BF16)


def _conv(u, b_in, w_dw, b_dw, ln_g, ln_b):
    tm = CONV_TM
    halo_blocks = tm // CONV_HALO
    vec = lambda n: pl.BlockSpec((1, n), lambda i: (0, 0))
    vmem = (2 * tm * 2 * CONV_WIDTH * 4 + 2 * CONV_HALO * 2 * CONV_WIDTH * 4
            + (2 * tm + CONV_HALO) * CONV_WIDTH * 4 + 2 * tm * CONV_WIDTH * 2
            + 6 * tm * CONV_WIDTH * 4 + (6 << 20))
    return pl.pallas_call(
        _conv_kernel,
        grid=(SEQ // tm,),
        in_specs=[
            pl.BlockSpec((tm, 2 * CONV_WIDTH), lambda i: (i, 0)),
            pl.BlockSpec((CONV_HALO, 2 * CONV_WIDTH), lambda i: (jnp.maximum(i * halo_blocks - 1, 0), 0)),
            vec(2 * CONV_WIDTH),
            pl.BlockSpec((CONV_KERNEL, CONV_WIDTH), lambda i: (0, 0)),
            vec(CONV_WIDTH), vec(CONV_WIDTH), vec(CONV_WIDTH),
        ],
        out_specs=pl.BlockSpec((tm, CONV_WIDTH), lambda i: (i, 0)),
        out_shape=jax.ShapeDtypeStruct((SEQ, CONV_WIDTH), BF16),
        scratch_shapes=[pltpu.VMEM((tm + CONV_HALO, CONV_WIDTH), F32), pltpu.VMEM((tm, CONV_WIDTH), F32)],
        compiler_params=_params(("parallel",), vmem),
        name="convbranch",
    )(u, u, b_in.reshape(1, -1), w_dw, b_dw.reshape(1, -1), ln_g.reshape(1, -1), ln_b.reshape(1, -1))


N_SLOPE_PARTS = 3
AUG_LANES = 128
POS_SPLIT = 256
assert POS_SPLIT <= 256 and ATT_TQ // POS_SPLIT <= 256
assert ATT_TQ % ATT_TK == 0 and ATT_TK % ATT_CB == 0 and ATT_CB % CHUNK == 0


def _aug_operand(n_rows, parts, key_side):
    lane = lax.broadcasted_iota(jnp.int32, (n_rows, AUG_LANES), 1)
    r = lax.broadcasted_iota(jnp.int32, (n_rows, AUG_LANES), 0)
    hi = ((r // POS_SPLIT) * POS_SPLIT).astype(F32)
    lo = (r % POS_SPLIT).astype(F32)
    piece = lane % N_SLOPE_PARTS
    part = jnp.where(piece == 0, parts[0], jnp.where(piece == 1, parts[1], parts[2]))
    zero = jnp.zeros((n_rows, AUG_LANES), F32)
    if key_side:
        vals = (-part, -part, hi, lo)
    else:
        vals = (hi, lo, part, part)
    out = zero
    for g, v in enumerate(vals):
        out = jnp.where(lane // N_SLOPE_PARTS == g, v, out)
    return out


def _diagonal_blocks():
    tq, tk, cb = ATT_TQ, ATT_TK, ATT_CB
    nqc = tq // cb
    out = []
    for d in range(tq // tk):
        for b in range(2 * nqc):
            q_lo = (b % nqc) * cb
            if q_lo + cb <= d * tk:
                continue
            out.append((d, b, None if q_lo >= (d + 1) * tk else 1 + (q_lo - d * tk) // cb))
    return out


def _attn_kernel(parts_ref, lam_ref, q_ref, k_ref, vt_ref, g_ref, o_ref,
                 qs_ref, kaug_ref, corr_ref, s_ref, p_ref, alpha_ref, m_ref, l_ref, acc_ref):
    tq, tk, cb = ATT_TQ, ATT_TK, ATT_CB
    nqc = tq // cb
    nb = 2 * nqc
    h = pl.program_id(0)
    i = pl.program_id(1)
    n_before = i * (tq // tk)
    parts = [parts_ref[h * N_SLOPE_PARTS + n] for n in range(N_SLOPE_PARTS)]
    sl = parts[0] + parts[1] + parts[2]

    @pl.when(i == 0)
    def _():
        kaug_ref[...] = _aug_operand(tk, parts, key_side=True).astype(BF16)
        qaug = _aug_operand(tq, parts, key_side=False).astype(BF16)
        qs_ref[0:tq, V_HEAD_DIM:] = qaug
        qs_ref[tq:, V_HEAD_DIM:] = qaug
        corr_ref[0] = jnp.zeros((tk, cb), F32)
        for c in range(tk // cb):
            krel = lax.broadcasted_iota(jnp.int32, (tk, cb), 0)
            qrel = lax.broadcasted_iota(jnp.int32, (tk, cb), 1) + c * cb
            after = jnp.maximum(krel - qrel, 0).astype(F32)
            allowed = (krel // CHUNK) <= (qrel // CHUNK)
            corr_ref[1 + c] = jnp.where(allowed, -2.0 * sl * after, NEG_INF)

    q = q_ref[...]
    lane = lax.broadcasted_iota(jnp.int32, (tq, V_HEAD_DIM), 1)
    zero = jnp.zeros_like(q)
    qs_ref[0:tq, 0:V_HEAD_DIM] = jnp.where(lane < QK_HEAD_DIM, q, zero)
    qs_ref[tq:, 0:V_HEAD_DIM] = jnp.where(lane >= QK_HEAD_DIM, q, zero)

    m_ref[...] = jnp.full_like(m_ref, NEG_INF)
    l_ref[...] = jnp.zeros_like(l_ref)
    acc_ref[...] = jnp.zeros_like(acc_ref)

    def scores(t, b, slot):
        k = k_ref[pl.ds(pl.multiple_of(t * tk, tk), tk), :]
        kx = jnp.concatenate([k, kaug_ref[...]], axis=1)
        s_ref[slot] = lax.dot_general(kx, qs_ref[b * cb:(b + 1) * cb, :], (((1,), (1,)), ((), ())),
                                      preferred_element_type=F32)

    def softmax(t, b, slot, corr):
        off = sl * (i * tq - t * tk).astype(F32)
        s = s_ref[slot]
        if corr is not None:
            s = s + corr_ref[corr]
        m_prev = m_ref[b]
        m_new = jnp.maximum(m_prev, jnp.max(s, axis=0, keepdims=True) - off)
        alpha = jnp.exp2(m_prev - m_new)
        p = jnp.exp2(s - (m_new + off))
        l_ref[b] = alpha * l_ref[b] + jnp.sum(p, axis=0, keepdims=True)
        m_ref[b] = m_new
        alpha_ref[slot] = alpha
        p_ref[slot] = p.astype(BF16)

    def values(t, b, slot):
        vt = vt_ref[:, pl.ds(pl.multiple_of(t * tk, tk), tk)]
        pv = jnp.dot(vt, p_ref[slot], preferred_element_type=F32)
        acc_ref[b] = alpha_ref[slot] * acc_ref[b] + pv

    blocks = _diagonal_blocks()
    lead = [blk for blk in blocks if blk[0] == 0][:ATT_SKEW_S]
    assert [blk[1] for blk in lead] == list(range(ATT_SKEW_S)) and nb % ATT_RING == 0

    def lead_corr(t, b):
        corr = lead[b][2]
        return None if corr is None else jnp.where(t == n_before, corr, 0)

    for b in range(ATT_SKEW_S):
        scores(0, b, b % ATT_RING)
    for b in range(ATT_SKEW_P):
        softmax(0, b, b % ATT_RING, lead_corr(0, b))

    tiles_per_iter = tq // tk

    def body(jj, carry):
        for u in range(tiles_per_iter):
            j = jj * tiles_per_iter + u
            last = u == tiles_per_iter - 1
            for b in range(nb):
                bs, bp = b + ATT_SKEW_S, b + ATT_SKEW_P
                scores(j + bs // nb, bs % nb, bs % ATT_RING)
                if bp < nb or not last:
                    softmax(j + bp // nb, bp % nb, bp % ATT_RING, None)
                else:
                    softmax(j + 1, bp % nb, bp % ATT_RING, lead_corr(j + 1, bp % nb))
                values(j, b, b % ATT_RING)
        return carry

    lax.fori_loop(0, i, body, 0)

    for n, (d, b, _) in enumerate(blocks):
        if n + ATT_SKEW_S < len(blocks):
            d2, b2, _ = blocks[n + ATT_SKEW_S]
            scores(n_before + d2, b2, (n + ATT_SKEW_S) % ATT_RING)
        if n + ATT_SKEW_P < len(blocks):
            d1, b1, corr1 = blocks[n + ATT_SKEW_P]
            softmax(n_before + d1, b1, (n + ATT_SKEW_P) % ATT_RING, corr1)
        values(n_before + d, b, n % ATT_RING)

    lam = lam_ref[0]
    for c in range(nqc):
        d = acc_ref[c] / l_ref[c] - lam * (acc_ref[nqc + c] / l_ref[nqc + c])
        ms = jnp.mean(d * d, axis=0, keepdims=True)
        y = d * lax.rsqrt(ms + EPS) * g_ref[...] * (1.0 - LAMBDA_INIT)
        o_ref[c * cb:(c + 1) * cb, :] = y.T.astype(BF16)


def _slope_parts():
    rest = (2.0 ** (-8.0 * (np.arange(N_HEADS, dtype=np.float64) + 1.0) / N_HEADS) * LOG2E).astype(np.float32)
    pieces = []
    for _ in range(N_SLOPE_PARTS):
        piece = rest.astype(jnp.bfloat16).astype(np.float32)
        pieces.append(piece)
        rest = rest - piece
    return np.stack(pieces, axis=1).reshape(-1)


def _attention(qk, vt, slope_parts, lam, subln_g):
    tq, tk, cb = ATT_TQ, ATT_TK, ATT_CB
    nh = N_HEADS
    nb = 2 * tq // cb
    smem = pl.BlockSpec(memory_space=pltpu.SMEM)
    scratch = [
        pltpu.VMEM((2 * tq, V_HEAD_DIM + AUG_LANES), BF16),
        pltpu.VMEM((tk, AUG_LANES), BF16),
        pltpu.VMEM((1 + tk // cb, tk, cb), F32),
        pltpu.VMEM((ATT_RING, tk, cb), F32),
        pltpu.VMEM((ATT_RING, tk, cb), BF16),
        pltpu.VMEM((ATT_RING, 1, cb), F32),
        pltpu.VMEM((nb, 1, cb), F32),
        pltpu.VMEM((nb, 1, cb), F32),
        pltpu.VMEM((nb, V_HEAD_DIM, cb), F32),
    ]
    vmem = (2 * 2 * SEQ * V_HEAD_DIM * 2 + 4 * tq * V_HEAD_DIM * 2
            + 2 * tq * (V_HEAD_DIM + AUG_LANES) * 2 + tk * AUG_LANES * 2 + (1 + tk // cb) * tk * cb * 4
            + ATT_RING * tk * cb * 6 + (ATT_RING + 2 * nb) * 8 * cb * 4
            + nb * V_HEAD_DIM * cb * 4 + 8 * tk * cb * 4 + (6 << 20))
    return pl.pallas_call(
        _attn_kernel,
        grid=(nh, SEQ // tq),
        in_specs=[
            smem, smem,
            pl.BlockSpec((tq, V_HEAD_DIM), lambda h, i: (i, h)),
            pl.BlockSpec((SEQ, V_HEAD_DIM), lambda h, i: (0, nh + h)),
            pl.BlockSpec((V_HEAD_DIM, SEQ), lambda h, i: (h, 0)),
            pl.BlockSpec((V_HEAD_DIM, 1), lambda h, i: (0, 0)),
        ],
        out_specs=pl.BlockSpec((tq, V_HEAD_DIM), lambda h, i: (i, h)),
        out_shape=jax.ShapeDtypeStruct((SEQ, ATTN_WIDTH), BF16),
        scratch_shapes=scratch,
        compiler_params=_params(("parallel", "arbitrary"), vmem),
        name="diffattn",
    )(slope_parts, lam, qk, qk, vt, subln_g.reshape(V_HEAD_DIM, 1))


def _outproj_kernel(x_ref, gate_ref, gpost_ref, yc_ref, ya_ref, wc_ref, wa_ref, o_ref):
    y = jnp.dot(yc_ref[...], wc_ref[...], preferred_element_type=F32)
    y = y + jnp.dot(ya_ref[...], wa_ref[...], preferred_element_type=F32)
    o_ref[...] = x_ref[...] + gate_ref[...] * _rms(y, gpost_ref[...])


def _outproj(x, mod, g_post, y_conv, y_attn, w_out):
    tm = PROJ_TM
    vmem = (2 * 2 * tm * D_MODEL * 4 + 2 * 2 * tm * CONV_WIDTH * 2 + 2 * D_MODEL * D_MODEL * 2
            + 3 * tm * D_MODEL * 4 + (6 << 20))
    return pl.pallas_call(
        _outproj_kernel,
        grid=(SEQ // tm,),
        in_specs=[
            pl.BlockSpec((tm, D_MODEL), lambda i: (i, 0)),
            pl.BlockSpec((1, D_MODEL), lambda i: (0, 5)),
            pl.BlockSpec((1, D_MODEL), lambda i: (0, 0)),
            pl.BlockSpec((tm, CONV_WIDTH), lambda i: (i, 0)),
            pl.BlockSpec((tm, ATTN_WIDTH), lambda i: (i, 0)),
            pl.BlockSpec((CONV_WIDTH, D_MODEL), lambda i: (0, 0)),
            pl.BlockSpec((ATTN_WIDTH, D_MODEL), lambda i: (1, 0)),
        ],
        out_specs=pl.BlockSpec((tm, D_MODEL), lambda i: (i, 0)),
        out_shape=jax.ShapeDtypeStruct((SEQ, D_MODEL), F32),
        compiler_params=_params(("parallel",), vmem),
        name="outproj",
    )(x, mod, g_post.reshape(1, D_MODEL), y_conv, y_attn, w_out, w_out)


def kernel(x, c, w_ada, b_ada, g_pre, g_post, w_ffn1_gate, w_ffn1_up, w_ffn1_down, w_in, b_in_conv, w_dw, b_dw, conv_ln_g, conv_ln_b, lam_q1, lam_k1, lam_q2, lam_k2, subln_g, w_out, w_ffn2_gate, w_ffn2_up, w_ffn2_down):
    bf = lambda w: w[0].astype(BF16)
    x2 = x.reshape(SEQ, D_MODEL)

    mod = _ada(c, w_ada[0], b_ada[0])

    x2 = _ffn(x2, mod, 0, g_pre[0, 0], g_post[0, 0], bf(w_ffn1_gate), bf(w_ffn1_up), bf(w_ffn1_down))

    w_in0 = w_in[0]
    n_uqk = 2 * CONV_WIDTH + 2 * ATTN_WIDTH
    u_conv, qk, vt = _inproj(x2, mod, g_pre[0, 1], w_in0[:, :n_uqk].astype(BF16),
                             w_in0[:, n_uqk:].T.astype(BF16))
    y_conv = _conv(u_conv, b_in_conv[0], w_dw[0], b_dw[0], conv_ln_g[0], conv_ln_b[0])

    lam = (jnp.exp(jnp.sum(lam_q1[0].astype(F32) * lam_k1[0].astype(F32)))
           - jnp.exp(jnp.sum(lam_q2[0].astype(F32) * lam_k2[0].astype(F32)))
           + LAMBDA_INIT).reshape(1)
    slope_parts = jnp.asarray(_slope_parts())
    y_attn = _attention(qk, vt, slope_parts, lam, subln_g[0])

    x2 = _outproj(x2, mod, g_post[0, 1], y_conv, y_attn, bf(w_out))

    x2 = _ffn(x2, mod, 6, g_pre[0, 2], g_post[0, 2], bf(w_ffn2_gate), bf(w_ffn2_up), bf(w_ffn2_down))
    return x2.reshape(1, SEQ, D_MODEL)
```

```python
import functools
import math

import jax
import jax.numpy as jnp
import numpy as np
from jax import lax
from jax.experimental import pallas as pl
from jax.experimental.pallas import tpu as pltpu

D_MODEL = 2048
SEQ = 8192
CHUNK = 64
CONV_WIDTH = D_MODEL // 2
ATTN_WIDTH = D_MODEL - CONV_WIDTH
CONV_KERNEL = 31
N_HEADS = 8
V_HEAD_DIM = ATTN_WIDTH // N_HEADS
QK_HEAD_DIM = V_HEAD_DIM // 2
D_FF = 5632
N_MOD = 9
IN_WIDTH = 2 * CONV_WIDTH + 3 * ATTN_WIDTH
EPS = 1e-6
NEG_INF = -1e30
LAMBDA_INIT = 0.8 - 0.6 * math.exp(-0.3 * 0)

F32 = jnp.float32
BF16 = jnp.bfloat16

V7X_VMEM_BUDGET_BYTES = 60 * 1024 * 1024

ADA_TN = 1024
FFN_TM = 1024
FFN_TF = 256
PROJ_TM = 512
PROJ_TN = 1024
CONV_TM = 256
CONV_HALO = 32
CONV_RC = 32
CONV_CC = 512
SUBLANES = 8
ATT_TQ = 1024
ATT_TK = 512
ATT_CB = 256
ATT_SKEW_S = 4
ATT_SKEW_P = 2
ATT_RING = 4


def _params(semantics, vmem_bytes, flags=None):
    return pltpu.CompilerParams(
        dimension_semantics=semantics,
        vmem_limit_bytes=min(int(vmem_bytes), V7X_VMEM_BUDGET_BYTES),
        flags=flags,
    )


def _rms(x, g):
    ms = jnp.mean(x * x, axis=-1, keepdims=True)
    return x * lax.rsqrt(ms + EPS) * g


def _ada_kernel(c_ref, w_ref, b_ref, o_ref):
    c = c_ref[...]
    s = (c * jax.nn.sigmoid(c)).astype(BF16)
    s8 = jnp.broadcast_to(s, (8, D_MODEL))
    r = jnp.dot(s8, w_ref[...].astype(BF16), preferred_element_type=F32)
    o_ref[...] = r[0:1, :] + b_ref[...]


def _ada(c, w, b):
    n = w.shape[1]
    return pl.pallas_call(
        _ada_kernel,
        grid=(n // ADA_TN,),
        in_specs=[
            pl.BlockSpec((1, D_MODEL), lambda j: (0, 0)),
            pl.BlockSpec((D_MODEL, ADA_TN), lambda j: (0, j)),
            pl.BlockSpec((1, ADA_TN), lambda j: (0, j)),
        ],
        out_specs=pl.BlockSpec((1, ADA_TN), lambda j: (0, j)),
        out_shape=jax.ShapeDtypeStruct((1, n), F32),
        compiler_params=_params(("parallel",), 2 * D_MODEL * ADA_TN * 4 + D_MODEL * ADA_TN * 2 + (8 << 20)),
        name="adaln",
    )(c, w, b.reshape(1, n))


def _ffn_kernel(x_ref, shift_ref, scale_ref, gate_ref, gpre_ref, gpost_ref,
                wg_ref, wu_ref, wd_ref, o_ref, h_ref):
    j = pl.program_id(1)

    @pl.when(j == 0)
    def _():
        y = _rms(x_ref[...], gpre_ref[...])
        h_ref[...] = (y * (1.0 + scale_ref[...]) + shift_ref[...]).astype(BF16)
        o_ref[...] = jnp.zeros_like(o_ref)

    h = h_ref[...]
    g = jnp.dot(h, wg_ref[...].astype(BF16), preferred_element_type=F32)
    u = jnp.dot(h, wu_ref[...].astype(BF16), preferred_element_type=F32)
    a = (g * jax.nn.sigmoid(g) * u).astype(BF16)
    o_ref[...] += jnp.dot(a, wd_ref[...].astype(BF16), preferred_element_type=F32)

    @pl.when(j == pl.num_programs(1) - 1)
    def _():
        r = _rms(o_ref[...], gpost_ref[...])
        o_ref[...] = x_ref[...] + 0.5 * gate_ref[...] * r


def _ffn(x, mod, mod_base, g_pre, g_post, wg, wu, wd):
    tm, tf = FFN_TM, FFN_TF
    row = lambda k: pl.BlockSpec((1, D_MODEL), lambda i, j, k=k: (0, k))
    vec = pl.BlockSpec((1, D_MODEL), lambda i, j: (0, 0))
    vmem = (2 * 2 * tm * D_MODEL * 4
            + tm * D_MODEL * 2
            + 2 * 3 * D_MODEL * tf * 4
            + 3 * D_MODEL * tf * 2
            + 3 * tm * tf * 4
            + (4 << 20))
    return pl.pallas_call(
        _ffn_kernel,
        grid=(SEQ // tm, D_FF // tf),
        in_specs=[
            pl.BlockSpec((tm, D_MODEL), lambda i, j: (i, 0)),
            row(mod_base), row(mod_base + 1), row(mod_base + 2),
            vec, vec,
            pl.BlockSpec((D_MODEL, tf), lambda i, j: (0, j)),
            pl.BlockSpec((D_MODEL, tf), lambda i, j: (0, j)),
            pl.BlockSpec((tf, D_MODEL), lambda i, j: (j, 0)),
        ],
        out_specs=pl.BlockSpec((tm, D_MODEL), lambda i, j: (i, 0)),
        out_shape=jax.ShapeDtypeStruct((SEQ, D_MODEL), F32),
        scratch_shapes=[pltpu.VMEM((tm, D_MODEL), BF16)],
        compiler_params=_params(("parallel", "arbitrary"), vmem),
        name="ffn",
    )(x, mod, mod, mod, g_pre.reshape(1, D_MODEL), g_post.reshape(1, D_MODEL), wg, wu, wd)


N_U_BLOCKS = 2 * CONV_WIDTH // PROJ_TN
N_QK_BLOCKS = 2 * ATTN_WIDTH // PROJ_TN
assert PROJ_TN == ATTN_WIDTH
LOG2E = math.log2(math.e)
Q_SCALE_LOG2 = QK_HEAD_DIM ** -0.5 * LOG2E


def _inproj_kernel(x_ref, shift_ref, scale_ref, gpre_ref, w_ref, u_ref, qk_ref, vt_ref, h_ref):
    j = pl.program_id(1)

    @pl.when(j == 0)
    def _():
        y = _rms(x_ref[...], gpre_ref[...])
        h_ref[...] = (y * (1.0 + scale_ref[...]) + shift_ref[...]).astype(BF16)

    @pl.when(j < N_U_BLOCKS)
    def _():
        u_ref[...] = jnp.dot(h_ref[...], w_ref[...], preferred_element_type=F32)

    @pl.when(jnp.logical_and(j >= N_U_BLOCKS, j < N_U_BLOCKS + N_QK_BLOCKS))
    def _():
        c = jnp.where(j == N_U_BLOCKS, Q_SCALE_LOG2, 1.0)
        qk_ref[...] = (jnp.dot(h_ref[...], w_ref[...], preferred_element_type=F32) * c).astype(BF16)

    @pl.when(j == N_U_BLOCKS + N_QK_BLOCKS)
    def _():
        v = jnp.dot(h_ref[...], w_ref[...], preferred_element_type=F32)
        vt_ref[...] = v.T.astype(BF16)


def _inproj(x, mod, g_pre, w_in):
    tm, tn = PROJ_TM, PROJ_TN
    n_main = N_U_BLOCKS + N_QK_BLOCKS
    vmem = (2 * tm * D_MODEL * 4 + tm * D_MODEL * 2 + 2 * D_MODEL * tn * 2
            + 2 * tm * tn * (4 + 2) + 2 * ATTN_WIDTH * tm * 2 + 3 * tm * tn * 4 + (6 << 20))
    return pl.pallas_call(
        _inproj_kernel,
        grid=(SEQ // tm, n_main + 1),
        in_specs=[
            pl.BlockSpec((tm, D_MODEL), lambda i, j: (i, 0)),
            pl.BlockSpec((1, D_MODEL), lambda i, j: (0, 3)),
            pl.BlockSpec((1, D_MODEL), lambda i, j: (0, 4)),
            pl.BlockSpec((1, D_MODEL), lambda i, j: (0, 0)),
            pl.BlockSpec((D_MODEL, tn), lambda i, j: (0, j)),
        ],
        out_specs=[
            pl.BlockSpec((tm, tn), lambda i, j: (i, jnp.minimum(j, N_U_BLOCKS - 1))),
            pl.BlockSpec((tm, tn), lambda i, j: (i, jnp.clip(j - N_U_BLOCKS, 0, N_QK_BLOCKS - 1))),
            pl.BlockSpec((ATTN_WIDTH, tm), lambda i, j: (0, i)),
        ],
        out_shape=[
            jax.ShapeDtypeStruct((SEQ, 2 * CONV_WIDTH), F32),
            jax.ShapeDtypeStruct((SEQ, 2 * ATTN_WIDTH), BF16),
            jax.ShapeDtypeStruct((ATTN_WIDTH, SEQ), BF16),
        ],
        scratch_shapes=[pltpu.VMEM((tm, D_MODEL), BF16)],
        compiler_params=_params(("parallel", "arbitrary"), vmem),
        name="inproj",
    )(x, mod, mod, g_pre.reshape(1, D_MODEL), w_in)


def _conv_kernel(u_ref, uprev_ref, bin_ref, wdw_ref, bdw_ref, lng_ref, lnb_ref, o_ref,
                 vbuf_ref, conv_ref, wb_ref):
    i = pl.program_id(0)
    b = bin_ref[...]

    def glu(u):
        u = u + b
        return u[:, :CONV_WIDTH] * jax.nn.sigmoid(u[:, CONV_WIDTH:])

    vbuf_ref[0, 0:CONV_HALO, :] = jnp.where(i > 0, glu(uprev_ref[...]), 0.0)
    vbuf_ref[0, CONV_HALO:, :] = glu(u_ref[...])
    n_shift = CONV_TM + CONV_HALO - SUBLANES
    for s in range(1, SUBLANES):
        vbuf_ref[s, 0:n_shift, :] = vbuf_ref[0, s:s + n_shift, :]

    for j in range(CONV_KERNEL):
        wb_ref[j] = jnp.broadcast_to(wdw_ref[j:j + 1, :], (SUBLANES, CONV_WIDTH))

    lead = CONV_HALO - (CONV_KERNEL - 1)
    for cb in range(CONV_WIDTH // CONV_CC):
        cs = slice(cb * CONV_CC, (cb + 1) * CONV_CC)
        for r in range(CONV_TM // CONV_RC):
            accs = [jnp.zeros((SUBLANES, CONV_CC), F32) for _ in range(CONV_RC // SUBLANES)]
            for j in range(CONV_KERNEL):
                s, lo = (lead + j) % SUBLANES, r * CONV_RC + (lead + j) // SUBLANES * SUBLANES
                w = wb_ref[j, :, cs]
                for q in range(len(accs)):
                    accs[q] = accs[q] + vbuf_ref[s, lo + q * SUBLANES:lo + (q + 1) * SUBLANES, cs] * w
            for q, acc in enumerate(accs):
                conv_ref[r * CONV_RC + q * SUBLANES:r * CONV_RC + (q + 1) * SUBLANES, cs] = acc

    v = conv_ref[...] + bdw_ref[...]
    mu = jnp.mean(v, axis=-1, keepdims=True)
    xc = v - mu
    var = jnp.mean(xc * xc, axis=-1, keepdims=True)
    y = xc * lax.rsqrt(var + EPS) * lng_ref[...] + lnb_ref[...]
    o_ref[...] = (y * jax.nn.sigmoid(y)).astype(BF16)


def _conv(u, b_in, w_dw, b_dw, ln_g, ln_b):
    tm = CONV_TM
    halo_blocks = tm // CONV_HALO
    vec = lambda n: pl.BlockSpec((1, n), lambda i: (0, 0))
    vmem = (2 * tm * 2 * CONV_WIDTH * 4 + 2 * CONV_HALO * 2 * CONV_WIDTH * 4
            + (SUBLANES * (tm + CONV_HALO) + tm) * CONV_WIDTH * 4 + 2 * tm * CONV_WIDTH * 2
            + 6 * tm * CONV_WIDTH * 4 + (6 << 20))
    return pl.pallas_call(
        _conv_kernel,
        grid=(SEQ // tm,),
        in_specs=[
            pl.BlockSpec((tm, 2 * CONV_WIDTH), lambda i: (i, 0)),
            pl.BlockSpec((CONV_HALO, 2 * CONV_WIDTH), lambda i: (jnp.maximum(i * halo_blocks - 1, 0), 0)),
            vec(2 * CONV_WIDTH),
            pl.BlockSpec((CONV_KERNEL, CONV_WIDTH), lambda i: (0, 0)),
            vec(CONV_WIDTH), vec(CONV_WIDTH), vec(CONV_WIDTH),
        ],
        out_specs=pl.BlockSpec((tm, CONV_WIDTH), lambda i: (i, 0)),
        out_shape=jax.ShapeDtypeStruct((SEQ, CONV_WIDTH), BF16),
        scratch_shapes=[pltpu.VMEM((SUBLANES, tm + CONV_HALO, CONV_WIDTH), F32),
                        pltpu.VMEM((tm, CONV_WIDTH), F32),
                        pltpu.VMEM((CONV_KERNEL, SUBLANES, CONV_WIDTH), F32)],
        compiler_params=_params(("parallel",), vmem),
        name="convbranch",
    )(u, u, b_in.reshape(1, -1), w_dw, b_dw.reshape(1, -1), ln_g.reshape(1, -1), ln_b.reshape(1, -1))


N_SLOPE_PARTS = 3
AUG_LANES = 128
POS_SPLIT = 256
assert POS_SPLIT <= 256 and ATT_TQ // POS_SPLIT <= 256
assert ATT_TQ % ATT_TK == 0 and ATT_TK % ATT_CB == 0 and ATT_CB % CHUNK == 0


def _aug_operand(n_rows, parts, key_side):
    lane = lax.broadcasted_iota(jnp.int32, (n_rows, AUG_LANES), 1)
    r = lax.broadcasted_iota(jnp.int32, (n_rows, AUG_LANES), 0)
    hi = ((r // POS_SPLIT) * POS_SPLIT).astype(F32)
    lo = (r % POS_SPLIT).astype(F32)
    piece = lane % N_SLOPE_PARTS
    part = jnp.where(piece == 0, parts[0], jnp.where(piece == 1, parts[1], parts[2]))
    zero = jnp.zeros((n_rows, AUG_LANES), F32)
    if key_side:
        vals = (-part, -part, hi, lo)
    else:
        vals = (hi, lo, part, part)
    out = zero
    for g, v in enumerate(vals):
        out = jnp.where(lane // N_SLOPE_PARTS == g, v, out)
    return out


def _diagonal_blocks():
    tq, tk, cb = ATT_TQ, ATT_TK, ATT_CB
    nqc = tq // cb
    out = []
    for d in range(tq // tk):
        for b in range(2 * nqc):
            q_lo = (b % nqc) * cb
            if q_lo + cb <= d * tk:
                continue
            out.append((d, b, None if q_lo >= (d + 1) * tk else 1 + (q_lo - d * tk) // cb))
    return out


def _attn_kernel(parts_ref, lam_ref, q_ref, k_ref, vt_ref, g_ref, o_ref,
                 qs_ref, kaug_ref, corr_ref, s_ref, p_ref, alpha_ref, m_ref, l_ref, acc_ref):
    tq, tk, cb = ATT_TQ, ATT_TK, ATT_CB
    nqc = tq // cb
    nb = 2 * nqc
    h = pl.program_id(0)
    i = pl.program_id(1)
    n_before = i * (tq // tk)
    parts = [parts_ref[h * N_SLOPE_PARTS + n] for n in range(N_SLOPE_PARTS)]
    sl = parts[0] + parts[1] + parts[2]

    @pl.when(i == 0)
    def _():
        kaug_ref[...] = _aug_operand(tk, parts, key_side=True).astype(BF16)
        qaug = _aug_operand(tq, parts, key_side=False).astype(BF16)
        qs_ref[0:tq, V_HEAD_DIM:] = qaug
        qs_ref[tq:, V_HEAD_DIM:] = qaug
        corr_ref[0] = jnp.zeros((tk, cb), F32)
        for c in range(tk // cb):
            krel = lax.broadcasted_iota(jnp.int32, (tk, cb), 0)
            qrel = lax.broadcasted_iota(jnp.int32, (tk, cb), 1) + c * cb
            after = jnp.maximum(krel - qrel, 0).astype(F32)
            allowed = (krel // CHUNK) <= (qrel // CHUNK)
            corr_ref[1 + c] = jnp.where(allowed, -2.0 * sl * after, NEG_INF)

    q = q_ref[...]
    lane = lax.broadcasted_iota(jnp.int32, (tq, V_HEAD_DIM), 1)
    zero = jnp.zeros_like(q)
    qs_ref[0:tq, 0:V_HEAD_DIM] = jnp.where(lane < QK_HEAD_DIM, q, zero)
    qs_ref[tq:, 0:V_HEAD_DIM] = jnp.where(lane >= QK_HEAD_DIM, q, zero)

    m_ref[...] = jnp.full_like(m_ref, NEG_INF)
    l_ref[...] = jnp.zeros_like(l_ref)
    acc_ref[...] = jnp.zeros_like(acc_ref)

    def scores(t, b, slot):
        k = k_ref[pl.ds(pl.multiple_of(t * tk, tk), tk), :]
        kx = jnp.concatenate([k, kaug_ref[...]], axis=1)
        s_ref[slot] = lax.dot_general(kx, qs_ref[b * cb:(b + 1) * cb, :], (((1,), (1,)), ((), ())),
                                      preferred_element_type=F32)

    def softmax(t, b, slot, corr):
        off = sl * (i * tq - t * tk).astype(F32)
        s = s_ref[slot]
        if corr is not None:
            s = s + corr_ref[corr]
        m_prev = m_ref[b]
        m_new = jnp.maximum(m_prev, jnp.max(s, axis=0, keepdims=True) - off)
        alpha = jnp.exp2(m_prev - m_new)
        p = jnp.exp2(s - (m_new + off))
        l_ref[b] = alpha * l_ref[b] + jnp.sum(p, axis=0, keepdims=True)
        m_ref[b] = m_new
        alpha_ref[slot] = alpha
        p_ref[slot] = p.astype(BF16)

    def value_tile(t):
        return vt_ref[:, pl.ds(pl.multiple_of(t * tk, tk), tk)]

    def values(vt, b, slot):
        pv = jnp.dot(vt, p_ref[slot], preferred_element_type=F32)
        acc_ref[b] = alpha_ref[slot] * acc_ref[b] + pv

    blocks = _diagonal_blocks()
    lead = [blk for blk in blocks if blk[0] == 0][:ATT_SKEW_S]
    assert [blk[1] for blk in lead] == list(range(ATT_SKEW_S)) and nb % ATT_RING == 0

    def lead_corr(t, b):
        corr = lead[b][2]
        return None if corr is None else jnp.where(t == n_before, corr, 0)

    for b in range(ATT_SKEW_S):
        scores(0, b, b % ATT_RING)
    for b in range(ATT_SKEW_P):
        softmax(0, b, b % ATT_RING, lead_corr(0, b))

    tiles_per_iter = tq // tk

    def body(jj, carry):
        for u in range(tiles_per_iter):
            j = jj * tiles_per_iter + u
            last = u == tiles_per_iter - 1
            vt = value_tile(j)
            for b in range(nb):
                bs, bp = b + ATT_SKEW_S, b + ATT_SKEW_P
                scores(j + bs // nb, bs % nb, bs % ATT_RING)
                if bp < nb or not last:
                    softmax(j + bp // nb, bp % nb, bp % ATT_RING, None)
                else:
                    softmax(j + 1, bp % nb, bp % ATT_RING, lead_corr(j + 1, bp % nb))
                values(vt, b, b % ATT_RING)
        return carry

    lax.fori_loop(0, i, body, 0)

    vts = {}
    for n, (d, b, _) in enumerate(blocks):
        if d not in vts:
            vts[d] = value_tile(n_before + d)
        if n + ATT_SKEW_S < len(blocks):
            d2, b2, _ = blocks[n + ATT_SKEW_S]
            scores(n_before + d2, b2, (n + ATT_SKEW_S) % ATT_RING)
        if n + ATT_SKEW_P < len(blocks):
            d1, b1, corr1 = blocks[n + ATT_SKEW_P]
            softmax(n_before + d1, b1, (n + ATT_SKEW_P) % ATT_RING, corr1)
        values(vts[d], b, n % ATT_RING)

    lam = lam_ref[0]
    for c in range(nqc):
        d = acc_ref[c] / l_ref[c] - lam * (acc_ref[nqc + c] / l_ref[nqc + c])
        ms = jnp.mean(d * d, axis=0, keepdims=True)
        y = d * lax.rsqrt(ms + EPS) * g_ref[...] * (1.0 - LAMBDA_INIT)
        o_ref[c * cb:(c + 1) * cb, :] = y.T.astype(BF16)


def _slope_parts():
    rest = (2.0 ** (-8.0 * (np.arange(N_HEADS, dtype=np.float64) + 1.0) / N_HEADS) * LOG2E).astype(np.float32)
    pieces = []
    for _ in range(N_SLOPE_PARTS):
        piece = rest.astype(jnp.bfloat16).astype(np.float32)
        pieces.append(piece)
        rest = rest - piece
    return np.stack(pieces, axis=1).reshape(-1)


def _attention(qk, vt, slope_parts, lam, subln_g):
    tq, tk, cb = ATT_TQ, ATT_TK, ATT_CB
    nh = N_HEADS
    nb = 2 * tq // cb
    smem = pl.BlockSpec(memory_space=pltpu.SMEM)
    scratch = [
        pltpu.VMEM((2 * tq, V_HEAD_DIM + AUG_LANES), BF16),
        pltpu.VMEM((tk, AUG_LANES), BF16),
        pltpu.VMEM((1 + tk // cb, tk, cb), F32),
        pltpu.VMEM((ATT_RING, tk, cb), F32),
        pltpu.VMEM((ATT_RING, tk, cb), BF16),
        pltpu.VMEM((ATT_RING, 1, cb), F32),
        pltpu.VMEM((nb, 1, cb), F32),
        pltpu.VMEM((nb, 1, cb), F32),
        pltpu.VMEM((nb, V_HEAD_DIM, cb), F32),
    ]
    vmem = (2 * 2 * SEQ * V_HEAD_DIM * 2 + 4 * tq * V_HEAD_DIM * 2
            + 2 * tq * (V_HEAD_DIM + AUG_LANES) * 2 + tk * AUG_LANES * 2 + (1 + tk // cb) * tk * cb * 4
            + ATT_RING * tk * cb * 6 + (ATT_RING + 2 * nb) * 8 * cb * 4
            + nb * V_HEAD_DIM * cb * 4 + 8 * tk * cb * 4 + (6 << 20))
    return pl.pallas_call(
        _attn_kernel,
        grid=(nh, SEQ // tq),
        in_specs=[
            smem, smem,
            pl.BlockSpec((tq, V_HEAD_DIM), lambda h, i: (i, h)),
            pl.BlockSpec((SEQ, V_HEAD_DIM), lambda h, i: (0, nh + h)),
            pl.BlockSpec((V_HEAD_DIM, SEQ), lambda h, i: (h, 0)),
            pl.BlockSpec((V_HEAD_DIM, 1), lambda h, i: (0, 0)),
        ],
        out_specs=pl.BlockSpec((tq, V_HEAD_DIM), lambda h, i: (i, h)),
        out_shape=jax.ShapeDtypeStruct((SEQ, ATTN_WIDTH), BF16),
        scratch_shapes=scratch,
        compiler_params=_params(("parallel", "arbitrary"), vmem),
        name="diffattn",
    )(slope_parts, lam, qk, qk, vt, subln_g.reshape(V_HEAD_DIM, 1))


def _outproj_kernel(x_ref, gate_ref, gpost_ref, yc_ref, ya_ref, wc_ref, wa_ref, o_ref):
    y = jnp.dot(yc_ref[...], wc_ref[...], preferred_element_type=F32)
    y = y + jnp.dot(ya_ref[...], wa_ref[...], preferred_element_type=F32)
    o_ref[...] = x_ref[...] + gate_ref[...] * _rms(y, gpost_ref[...])


def _outproj(x, mod, g_post, y_conv, y_attn, w_out):
    tm = PROJ_TM
    vmem = (2 * 2 * tm * D_MODEL * 4 + 2 * 2 * tm * CONV_WIDTH * 2 + 2 * D_MODEL * D_MODEL * 2
            + 3 * tm * D_MODEL * 4 + (6 << 20))
    return pl.pallas_call(
        _outproj_kernel,
        grid=(SEQ // tm,),
        in_specs=[
            pl.BlockSpec((tm, D_MODEL), lambda i: (i, 0)),
            pl.BlockSpec((1, D_MODEL), lambda i: (0, 5)),
            pl.BlockSpec((1, D_MODEL), lambda i: (0, 0)),
            pl.BlockSpec((tm, CONV_WIDTH), lambda i: (i, 0)),
            pl.BlockSpec((tm, ATTN_WIDTH), lambda i: (i, 0)),
            pl.BlockSpec((CONV_WIDTH, D_MODEL), lambda i: (0, 0)),
            pl.BlockSpec((ATTN_WIDTH, D_MODEL), lambda i: (1, 0)),
        ],
        out_specs=pl.BlockSpec((tm, D_MODEL), lambda i: (i, 0)),
        out_shape=jax.ShapeDtypeStruct((SEQ, D_MODEL), F32),
        compiler_params=_params(("parallel",), vmem),
        name="outproj",
    )(x, mod, g_post.reshape(1, D_MODEL), y_conv, y_attn, w_out, w_out)


def kernel(x, c, w_ada, b_ada, g_pre, g_post, w_ffn1_gate, w_ffn1_up, w_ffn1_down, w_in, b_in_conv, w_dw, b_dw, conv_ln_g, conv_ln_b, lam_q1, lam_k1, lam_q2, lam_k2, subln_g, w_out, w_ffn2_gate, w_ffn2_up, w_ffn2_down):
    bf = lambda w: w[0].astype(BF16)
    x2 = x.reshape(SEQ, D_MODEL)

    mod = _ada(c, w_ada[0], b_ada[0])

    x2 = _ffn(x2, mod, 0, g_pre[0, 0], g_post[0, 0], w_ffn1_gate[0], w_ffn1_up[0], w_ffn1_down[0])

    u_conv, qk, vt = _inproj(x2, mod, g_pre[0, 1], bf(w_in))
    y_conv = _conv(u_conv, b_in_conv[0], w_dw[0], b_dw[0], conv_ln_g[0], conv_ln_b[0])

    lam = (jnp.exp(jnp.sum(lam_q1[0].astype(F32) * lam_k1[0].astype(F32)))
           - jnp.exp(jnp.sum(lam_q2[0].astype(F32) * lam_k2[0].astype(F32)))
           + LAMBDA_INIT).reshape(1)
    slope_parts = jnp.asarray(_slope_parts())
    y_attn = _attention(qk, vt, slope_parts, lam, subln_g[0])

    x2 = _outproj(x2, mod, g_post[0, 1], y_conv, y_attn, bf(w_out))

    x2 = _ffn(x2, mod, 6, g_pre[0, 2], g_post[0, 2], w_ffn2_gate[0], w_ffn2_up[0], w_ffn2_down[0])
    return x2.reshape(1, SEQ, D_MODEL)
```

```python
import functools
import math

import jax
import jax.numpy as jnp
import numpy as np
from jax import lax
from jax.experimental import pallas as pl
from jax.experimental.pallas import tpu as pltpu

D_MODEL = 2048
SEQ = 8192
CHUNK = 64
CONV_WIDTH = D_MODEL // 2
ATTN_WIDTH = D_MODEL - CONV_WIDTH
CONV_KERNEL = 31
N_HEADS = 8
V_HEAD_DIM = ATTN_WIDTH // N_HEADS
QK_HEAD_DIM = V_HEAD_DIM // 2
D_FF = 5632
N_MOD = 9
IN_WIDTH = 2 * CONV_WIDTH + 3 * ATTN_WIDTH
EPS = 1e-6
NEG_INF = -1e30
LAMBDA_INIT = 0.8 - 0.6 * math.exp(-0.3 * 0)

F32 = jnp.float32
BF16 = jnp.bfloat16

V7X_VMEM_BUDGET_BYTES = 60 * 1024 * 1024

ADA_TN = 1024
FFN_TM = 1024
FFN_TF = 256
INPROJ_TM = 1024
OUTPROJ_TM = 512
PROJ_TN = 1024
CONV_TM = 256
CONV_HALO = 32
CONV_RC = 32
CONV_CC = 512
SUBLANES = 8
ATT_TQ = 1024
ATT_TK = 512
ATT_CB = 256
ATT_SKEW_S = 4
ATT_SKEW_P = 2
ATT_RING = 4


def _params(semantics, vmem_bytes, flags=None):
    return pltpu.CompilerParams(
        dimension_semantics=semantics,
        vmem_limit_bytes=min(int(vmem_bytes), V7X_VMEM_BUDGET_BYTES),
        flags=flags,
    )


ROW_CHUNK = 16


def _norm_rows(src_ref, scale_ref, dst_ref, finish):
    for c in range(src_ref.shape[0] // ROW_CHUNK):
        rows = slice(c * ROW_CHUNK, (c + 1) * ROW_CHUNK)
        x = src_ref[rows, :]
        ms = jnp.mean(x * x, axis=-1, keepdims=True)
        dst_ref[rows, :] = finish(rows, x * lax.rsqrt(ms + EPS) * scale_ref[...]).astype(dst_ref.dtype)


def _modulated_norm(x_ref, gpre_ref, scale_ref, shift_ref, h_ref, vec_ref):
    vec_ref[...] = gpre_ref[...] * (1.0 + scale_ref[...])
    _norm_rows(x_ref, vec_ref, h_ref, lambda rows, y: y + shift_ref[...])


def _gated_residual(x_ref, y_ref, gpost_ref, gate_ref, gate_factor, o_ref, vec_ref):
    vec_ref[...] = gpost_ref[...] * (gate_factor * gate_ref[...])
    _norm_rows(y_ref, vec_ref, o_ref, lambda rows, r: x_ref[rows, :] + r)


def _ada_kernel(c_ref, w_ref, b_ref, o_ref):
    c = c_ref[...]
    s = (c * jax.nn.sigmoid(c)).astype(BF16)
    s8 = jnp.broadcast_to(s, (8, D_MODEL))
    r = jnp.dot(s8, w_ref[...].astype(BF16), preferred_element_type=F32)
    o_ref[...] = r[0:1, :] + b_ref[...]


def _ada(c, w, b):
    n = w.shape[1]
    return pl.pallas_call(
        _ada_kernel,
        grid=(n // ADA_TN,),
        in_specs=[
            pl.BlockSpec((1, D_MODEL), lambda j: (0, 0)),
            pl.BlockSpec((D_MODEL, ADA_TN), lambda j: (0, j)),
            pl.BlockSpec((1, ADA_TN), lambda j: (0, j)),
        ],
        out_specs=pl.BlockSpec((1, ADA_TN), lambda j: (0, j)),
        out_shape=jax.ShapeDtypeStruct((1, n), F32),
        compiler_params=_params(("parallel",), 2 * D_MODEL * ADA_TN * 4 + D_MODEL * ADA_TN * 2 + (8 << 20)),
        name="adaln",
    )(c, w, b.reshape(1, n))


def _ffn_kernel(x_ref, shift_ref, scale_ref, gate_ref, gpre_ref, gpost_ref,
                wg_ref, wu_ref, wd_ref, o_ref, h_ref, vec_ref):
    j = pl.program_id(1)

    @pl.when(j == 0)
    def _():
        _modulated_norm(x_ref, gpre_ref, scale_ref, shift_ref, h_ref, vec_ref)
        o_ref[...] = jnp.zeros_like(o_ref)

    h = h_ref[...]
    g = jnp.dot(h, wg_ref[...].astype(BF16), preferred_element_type=F32)
    u = jnp.dot(h, wu_ref[...].astype(BF16), preferred_element_type=F32)
    a = (g * jax.nn.sigmoid(g) * u).astype(BF16)
    o_ref[...] += jnp.dot(a, wd_ref[...].astype(BF16), preferred_element_type=F32)

    @pl.when(j == pl.num_programs(1) - 1)
    def _():
        _gated_residual(x_ref, o_ref, gpost_ref, gate_ref, 0.5, o_ref, vec_ref)


def _ffn(x, mod, mod_base, g_pre, g_post, wg, wu, wd):
    tm, tf = FFN_TM, FFN_TF
    row = lambda k: pl.BlockSpec((1, D_MODEL), lambda i, j, k=k: (0, k))
    vec = pl.BlockSpec((1, D_MODEL), lambda i, j: (0, 0))
    vmem = (2 * 2 * tm * D_MODEL * 4
            + tm * D_MODEL * 2
            + 2 * 3 * D_MODEL * tf * 4
            + 3 * D_MODEL * tf * 2
            + 3 * tm * tf * 4
            + (4 << 20))
    return pl.pallas_call(
        _ffn_kernel,
        grid=(SEQ // tm, D_FF // tf),
        in_specs=[
            pl.BlockSpec((tm, D_MODEL), lambda i, j: (i, 0)),
            row(mod_base), row(mod_base + 1), row(mod_base + 2),
            vec, vec,
            pl.BlockSpec((D_MODEL, tf), lambda i, j: (0, j)),
            pl.BlockSpec((D_MODEL, tf), lambda i, j: (0, j)),
            pl.BlockSpec((tf, D_MODEL), lambda i, j: (j, 0)),
        ],
        out_specs=pl.BlockSpec((tm, D_MODEL), lambda i, j: (i, 0)),
        out_shape=jax.ShapeDtypeStruct((SEQ, D_MODEL), F32),
        scratch_shapes=[pltpu.VMEM((tm, D_MODEL), BF16), pltpu.VMEM((1, D_MODEL), F32)],
        compiler_params=_params(("parallel", "arbitrary"), vmem),
        name="ffn",
    )(x, mod, mod, mod, g_pre.reshape(1, D_MODEL), g_post.reshape(1, D_MODEL), wg, wu, wd)


N_U_BLOCKS = 2 * CONV_WIDTH // PROJ_TN
N_QK_BLOCKS = 2 * ATTN_WIDTH // PROJ_TN
assert PROJ_TN == ATTN_WIDTH
LOG2E = math.log2(math.e)
Q_SCALE_LOG2 = QK_HEAD_DIM ** -0.5 * LOG2E


def _inproj_kernel(x_ref, shift_ref, scale_ref, gpre_ref, w_ref, u_ref, qk_ref, vt_ref, h_ref, vec_ref):
    j = pl.program_id(1)

    @pl.when(j == 0)
    def _():
        _modulated_norm(x_ref, gpre_ref, scale_ref, shift_ref, h_ref, vec_ref)

    @pl.when(j < N_U_BLOCKS)
    def _():
        u_ref[...] = jnp.dot(h_ref[...], w_ref[...], preferred_element_type=F32)

    @pl.when(jnp.logical_and(j >= N_U_BLOCKS, j < N_U_BLOCKS + N_QK_BLOCKS))
    def _():
        c = jnp.where(j == N_U_BLOCKS, Q_SCALE_LOG2, 1.0)
        qk_ref[...] = (jnp.dot(h_ref[...], w_ref[...], preferred_element_type=F32) * c).astype(BF16)

    @pl.when(j == N_U_BLOCKS + N_QK_BLOCKS)
    def _():
        v = jnp.dot(h_ref[...], w_ref[...], preferred_element_type=F32)
        vt_ref[...] = v.T.astype(BF16)


def _inproj(x, mod, g_pre, w_in):
    tm, tn = INPROJ_TM, PROJ_TN
    n_main = N_U_BLOCKS + N_QK_BLOCKS
    vmem = (2 * tm * D_MODEL * 4 + tm * D_MODEL * 2 + 2 * D_MODEL * tn * 2
            + 2 * tm * tn * (4 + 2) + 2 * ATTN_WIDTH * tm * 2 + 2 * tm * tn * 4 + (4 << 20))
    return pl.pallas_call(
        _inproj_kernel,
        grid=(SEQ // tm, n_main + 1),
        in_specs=[
            pl.BlockSpec((tm, D_MODEL), lambda i, j: (i, 0)),
            pl.BlockSpec((1, D_MODEL), lambda i, j: (0, 3)),
            pl.BlockSpec((1, D_MODEL), lambda i, j: (0, 4)),
            pl.BlockSpec((1, D_MODEL), lambda i, j: (0, 0)),
            pl.BlockSpec((D_MODEL, tn), lambda i, j: (0, j)),
        ],
        out_specs=[
            pl.BlockSpec((tm, tn), lambda i, j: (i, jnp.minimum(j, N_U_BLOCKS - 1))),
            pl.BlockSpec((tm, tn), lambda i, j: (i, jnp.clip(j - N_U_BLOCKS, 0, N_QK_BLOCKS - 1))),
            pl.BlockSpec((ATTN_WIDTH, tm), lambda i, j: (0, i)),
        ],
        out_shape=[
            jax.ShapeDtypeStruct((SEQ, 2 * CONV_WIDTH), F32),
            jax.ShapeDtypeStruct((SEQ, 2 * ATTN_WIDTH), BF16),
            jax.ShapeDtypeStruct((ATTN_WIDTH, SEQ), BF16),
        ],
        scratch_shapes=[pltpu.VMEM((tm, D_MODEL), BF16), pltpu.VMEM((1, D_MODEL), F32)],
        compiler_params=_params(("parallel", "arbitrary"), vmem),
        name="inproj",
    )(x, mod, mod, g_pre.reshape(1, D_MODEL), w_in)


def _conv_kernel(u_ref, uprev_ref, bin_ref, wdw_ref, bdw_ref, lng_ref, lnb_ref, o_ref,
                 vbuf_ref, conv_ref, wb_ref):
    i = pl.program_id(0)
    b = bin_ref[...]

    def glu(u):
        u = u + b
        return u[:, :CONV_WIDTH] * jax.nn.sigmoid(u[:, CONV_WIDTH:])

    vbuf_ref[0, 0:CONV_HALO, :] = jnp.where(i > 0, glu(uprev_ref[...]), 0.0)
    vbuf_ref[0, CONV_HALO:, :] = glu(u_ref[...])
    n_shift = CONV_TM + CONV_HALO - SUBLANES
    for s in range(1, SUBLANES):
        vbuf_ref[s, 0:n_shift, :] = vbuf_ref[0, s:s + n_shift, :]

    for j in range(CONV_KERNEL):
        wb_ref[j] = jnp.broadcast_to(wdw_ref[j:j + 1, :], (SUBLANES, CONV_WIDTH))

    lead = CONV_HALO - (CONV_KERNEL - 1)
    for cb in range(CONV_WIDTH // CONV_CC):
        cs = slice(cb * CONV_CC, (cb + 1) * CONV_CC)
        for r in range(CONV_TM // CONV_RC):
            accs = [jnp.zeros((SUBLANES, CONV_CC), F32) for _ in range(CONV_RC // SUBLANES)]
            for j in range(CONV_KERNEL):
                s, lo = (lead + j) % SUBLANES, r * CONV_RC + (lead + j) // SUBLANES * SUBLANES
                w = wb_ref[j, :, cs]
                for q in range(len(accs)):
                    accs[q] = accs[q] + vbuf_ref[s, lo + q * SUBLANES:lo + (q + 1) * SUBLANES, cs] * w
            for q, acc in enumerate(accs):
                conv_ref[r * CONV_RC + q * SUBLANES:r * CONV_RC + (q + 1) * SUBLANES, cs] = acc

    v = conv_ref[...] + bdw_ref[...]
    mu = jnp.mean(v, axis=-1, keepdims=True)
    xc = v - mu
    var = jnp.mean(xc * xc, axis=-1, keepdims=True)
    y = xc * lax.rsqrt(var + EPS) * lng_ref[...] + lnb_ref[...]
    o_ref[...] = (y * jax.nn.sigmoid(y)).astype(BF16)


def _conv(u, b_in, w_dw, b_dw, ln_g, ln_b):
    tm = CONV_TM
    halo_blocks = tm // CONV_HALO
    vec = lambda n: pl.BlockSpec((1, n), lambda i: (0, 0))
    vmem = (2 * tm * 2 * CONV_WIDTH * 4 + 2 * CONV_HALO * 2 * CONV_WIDTH * 4
            + (SUBLANES * (tm + CONV_HALO) + tm) * CONV_WIDTH * 4 + 2 * tm * CONV_WIDTH * 2
            + 6 * tm * CONV_WIDTH * 4 + (6 << 20))
    return pl.pallas_call(
        _conv_kernel,
        grid=(SEQ // tm,),
        in_specs=[
            pl.BlockSpec((tm, 2 * CONV_WIDTH), lambda i: (i, 0)),
            pl.BlockSpec((CONV_HALO, 2 * CONV_WIDTH), lambda i: (jnp.maximum(i * halo_blocks - 1, 0), 0)),
            vec(2 * CONV_WIDTH),
            pl.BlockSpec((CONV_KERNEL, CONV_WIDTH), lambda i: (0, 0)),
            vec(CONV_WIDTH), vec(CONV_WIDTH), vec(CONV_WIDTH),
        ],
        out_specs=pl.BlockSpec((tm, CONV_WIDTH), lambda i: (i, 0)),
        out_shape=jax.ShapeDtypeStruct((SEQ, CONV_WIDTH), BF16),
        scratch_shapes=[pltpu.VMEM((SUBLANES, tm + CONV_HALO, CONV_WIDTH), F32),
                        pltpu.VMEM((tm, CONV_WIDTH), F32),
                        pltpu.VMEM((CONV_KERNEL, SUBLANES, CONV_WIDTH), F32)],
        compiler_params=_params(("parallel",), vmem),
        name="convbranch",
    )(u, u, b_in.reshape(1, -1), w_dw, b_dw.reshape(1, -1), ln_g.reshape(1, -1), ln_b.reshape(1, -1))


N_SLOPE_PARTS = 3
AUG_LANES = 128
POS_SPLIT = 256
assert POS_SPLIT <= 256 and ATT_TQ // POS_SPLIT <= 256
assert ATT_TQ % ATT_TK == 0 and ATT_TK % ATT_CB == 0 and ATT_CB % CHUNK == 0


def _aug_operand(n_rows, parts, key_side):
    lane = lax.broadcasted_iota(jnp.int32, (n_rows, AUG_LANES), 1)
    r = lax.broadcasted_iota(jnp.int32, (n_rows, AUG_LANES), 0)
    hi = ((r // POS_SPLIT) * POS_SPLIT).astype(F32)
    lo = (r % POS_SPLIT).astype(F32)
    piece = lane % N_SLOPE_PARTS
    part = jnp.where(piece == 0, parts[0], jnp.where(piece == 1, parts[1], parts[2]))
    zero = jnp.zeros((n_rows, AUG_LANES), F32)
    if key_side:
        vals = (-part, -part, hi, lo)
    else:
        vals = (hi, lo, part, part)
    out = zero
    for g, v in enumerate(vals):
        out = jnp.where(lane // N_SLOPE_PARTS == g, v, out)
    return out


def _diagonal_blocks():
    tq, tk, cb = ATT_TQ, ATT_TK, ATT_CB
    nqc = tq // cb
    out = []
    for d in range(tq // tk):
        for b in range(2 * nqc):
            q_lo = (b % nqc) * cb
            if q_lo + cb <= d * tk:
                continue
            out.append((d, b, None if q_lo >= (d + 1) * tk else 1 + (q_lo - d * tk) // cb))
    return out


def _attn_kernel(parts_ref, lam_ref, q_ref, k_ref, vt_ref, g_ref, o_ref,
                 qs_ref, kaug_ref, corr_ref, s_ref, p_ref, alpha_ref, m_ref, l_ref, acc_ref):
    tq, tk, cb = ATT_TQ, ATT_TK, ATT_CB
    nqc = tq // cb
    nb = 2 * nqc
    h = pl.program_id(0)
    i = pl.program_id(1)
    n_before = i * (tq // tk)
    parts = [parts_ref[h * N_SLOPE_PARTS + n] for n in range(N_SLOPE_PARTS)]
    sl = parts[0] + parts[1] + parts[2]

    @pl.when(i == 0)
    def _():
        kaug_ref[...] = _aug_operand(tk, parts, key_side=True).astype(BF16)
        qaug = _aug_operand(tq, parts, key_side=False).astype(BF16)
        qs_ref[0:tq, V_HEAD_DIM:] = qaug
        qs_ref[tq:, V_HEAD_DIM:] = qaug
        corr_ref[0] = jnp.zeros((tk, cb), F32)
        for c in range(tk // cb):
            krel = lax.broadcasted_iota(jnp.int32, (tk, cb), 0)
            qrel = lax.broadcasted_iota(jnp.int32, (tk, cb), 1) + c * cb
            after = jnp.maximum(krel - qrel, 0).astype(F32)
            allowed = (krel // CHUNK) <= (qrel // CHUNK)
            corr_ref[1 + c] = jnp.where(allowed, -2.0 * sl * after, NEG_INF)

    q = q_ref[...]
    lane = lax.broadcasted_iota(jnp.int32, (tq, V_HEAD_DIM), 1)
    zero = jnp.zeros_like(q)
    qs_ref[0:tq, 0:V_HEAD_DIM] = jnp.where(lane < QK_HEAD_DIM, q, zero)
    qs_ref[tq:, 0:V_HEAD_DIM] = jnp.where(lane >= QK_HEAD_DIM, q, zero)

    m_ref[...] = jnp.full_like(m_ref, NEG_INF)
    l_ref[...] = jnp.zeros_like(l_ref)
    acc_ref[...] = jnp.zeros_like(acc_ref)

    def scores(t, b, slot):
        k = k_ref[pl.ds(pl.multiple_of(t * tk, tk), tk), :]
        kx = jnp.concatenate([k, kaug_ref[...]], axis=1)
        s_ref[slot] = lax.dot_general(kx, qs_ref[b * cb:(b + 1) * cb, :], (((1,), (1,)), ((), ())),
                                      preferred_element_type=F32)

    def softmax(t, b, slot, corr):
        off = sl * (i * tq - t * tk).astype(F32)
        s = s_ref[slot]
        if corr is not None:
            s = s + corr_ref[corr]
        m_prev = m_ref[b]
        m_new = jnp.maximum(m_prev, jnp.max(s, axis=0, keepdims=True) - off)
        alpha = jnp.exp2(m_prev - m_new)
        p = jnp.exp2(s - (m_new + off))
        l_ref[b] = alpha * l_ref[b] + jnp.sum(p, axis=0, keepdims=True)
        m_ref[b] = m_new
        alpha_ref[slot] = alpha
        p_ref[slot] = p.astype(BF16)

    def value_tile(t):
        return vt_ref[:, pl.ds(pl.multiple_of(t * tk, tk), tk)]

    def values(vt, b, slot):
        pv = jnp.dot(vt, p_ref[slot], preferred_element_type=F32)
        acc_ref[b] = alpha_ref[slot] * acc_ref[b] + pv

    blocks = _diagonal_blocks()
    lead = [blk for blk in blocks if blk[0] == 0][:ATT_SKEW_S]
    assert [blk[1] for blk in lead] == list(range(ATT_SKEW_S)) and nb % ATT_RING == 0

    def lead_corr(t, b):
        corr = lead[b][2]
        return None if corr is None else jnp.where(t == n_before, corr, 0)

    for b in range(ATT_SKEW_S):
        scores(0, b, b % ATT_RING)
    for b in range(ATT_SKEW_P):
        softmax(0, b, b % ATT_RING, lead_corr(0, b))

    tiles_per_iter = tq // tk

    def body(jj, carry):
        for u in range(tiles_per_iter):
            j = jj * tiles_per_iter + u
            last = u == tiles_per_iter - 1
            vt = value_tile(j)
            for b in range(nb):
                bs, bp = b + ATT_SKEW_S, b + ATT_SKEW_P
                scores(j + bs // nb, bs % nb, bs % ATT_RING)
                if bp < nb or not last:
                    softmax(j + bp // nb, bp % nb, bp % ATT_RING, None)
                else:
                    softmax(j + 1, bp % nb, bp % ATT_RING, lead_corr(j + 1, bp % nb))
                values(vt, b, b % ATT_RING)
        return carry

    lax.fori_loop(0, i, body, 0)

    vts = {}
    for n, (d, b, _) in enumerate(blocks):
        if d not in vts:
            vts[d] = value_tile(n_before + d)
        if n + ATT_SKEW_S < len(blocks):
            d2, b2, _ = blocks[n + ATT_SKEW_S]
            scores(n_before + d2, b2, (n + ATT_SKEW_S) % ATT_RING)
        if n + ATT_SKEW_P < len(blocks):
            d1, b1, corr1 = blocks[n + ATT_SKEW_P]
            softmax(n_before + d1, b1, (n + ATT_SKEW_P) % ATT_RING, corr1)
        values(vts[d], b, n % ATT_RING)

    lam = lam_ref[0]
    for c in range(nqc):
        d = acc_ref[c] / l_ref[c] - lam * (acc_ref[nqc + c] / l_ref[nqc + c])
        ms = jnp.mean(d * d, axis=0, keepdims=True)
        y = d * lax.rsqrt(ms + EPS) * g_ref[...] * (1.0 - LAMBDA_INIT)
        o_ref[c * cb:(c + 1) * cb, :] = y.T.astype(BF16)


def _slope_parts():
    rest = (2.0 ** (-8.0 * (np.arange(N_HEADS, dtype=np.float64) + 1.0) / N_HEADS) * LOG2E).astype(np.float32)
    pieces = []
    for _ in range(N_SLOPE_PARTS):
        piece = rest.astype(jnp.bfloat16).astype(np.float32)
        pieces.append(piece)
        rest = rest - piece
    return np.stack(pieces, axis=1).reshape(-1)


def _attention(qk, vt, slope_parts, lam, subln_g):
    tq, tk, cb = ATT_TQ, ATT_TK, ATT_CB
    nh = N_HEADS
    nb = 2 * tq // cb
    smem = pl.BlockSpec(memory_space=pltpu.SMEM)
    scratch = [
        pltpu.VMEM((2 * tq, V_HEAD_DIM + AUG_LANES), BF16),
        pltpu.VMEM((tk, AUG_LANES), BF16),
        pltpu.VMEM((1 + tk // cb, tk, cb), F32),
        pltpu.VMEM((ATT_RING, tk, cb), F32),
        pltpu.VMEM((ATT_RING, tk, cb), BF16),
        pltpu.VMEM((ATT_RING, 1, cb), F32),
        pltpu.VMEM((nb, 1, cb), F32),
        pltpu.VMEM((nb, 1, cb), F32),
        pltpu.VMEM((nb, V_HEAD_DIM, cb), F32),
    ]
    vmem = (2 * 2 * SEQ * V_HEAD_DIM * 2 + 4 * tq * V_HEAD_DIM * 2
            + 2 * tq * (V_HEAD_DIM + AUG_LANES) * 2 + tk * AUG_LANES * 2 + (1 + tk // cb) * tk * cb * 4
            + ATT_RING * tk * cb * 6 + (ATT_RING + 2 * nb) * 8 * cb * 4
            + nb * V_HEAD_DIM * cb * 4 + 8 * tk * cb * 4 + (6 << 20))
    return pl.pallas_call(
        _attn_kernel,
        grid=(nh, SEQ // tq),
        in_specs=[
            smem, smem,
            pl.BlockSpec((tq, V_HEAD_DIM), lambda h, i: (i, h)),
            pl.BlockSpec((SEQ, V_HEAD_DIM), lambda h, i: (0, nh + h)),
            pl.BlockSpec((V_HEAD_DIM, SEQ), lambda h, i: (h, 0)),
            pl.BlockSpec((V_HEAD_DIM, 1), lambda h, i: (0, 0)),
        ],
        out_specs=pl.BlockSpec((tq, V_HEAD_DIM), lambda h, i: (i, h)),
        out_shape=jax.ShapeDtypeStruct((SEQ, ATTN_WIDTH), BF16),
        scratch_shapes=scratch,
        compiler_params=_params(("parallel", "arbitrary"), vmem),
        name="diffattn",
    )(slope_parts, lam, qk, qk, vt, subln_g.reshape(V_HEAD_DIM, 1))


def _outproj_kernel(x_ref, gate_ref, gpost_ref, yc_ref, ya_ref, wc_ref, wa_ref, o_ref, vec_ref):
    y = jnp.dot(yc_ref[...], wc_ref[...], preferred_element_type=F32)
    o_ref[...] = y + jnp.dot(ya_ref[...], wa_ref[...], preferred_element_type=F32)
    _gated_residual(x_ref, o_ref, gpost_ref, gate_ref, 1.0, o_ref, vec_ref)


def _outproj(x, mod, g_post, y_conv, y_attn, w_out):
    tm = OUTPROJ_TM
    vmem = (2 * 2 * tm * D_MODEL * 4 + 2 * 2 * tm * CONV_WIDTH * 2 + 2 * D_MODEL * D_MODEL * 2
            + 3 * tm * D_MODEL * 4 + (6 << 20))
    return pl.pallas_call(
        _outproj_kernel,
        grid=(SEQ // tm,),
        in_specs=[
            pl.BlockSpec((tm, D_MODEL), lambda i: (i, 0)),
            pl.BlockSpec((1, D_MODEL), lambda i: (0, 5)),
            pl.BlockSpec((1, D_MODEL), lambda i: (0, 0)),
            pl.BlockSpec((tm, CONV_WIDTH), lambda i: (i, 0)),
            pl.BlockSpec((tm, ATTN_WIDTH), lambda i: (i, 0)),
            pl.BlockSpec((CONV_WIDTH, D_MODEL), lambda i: (0, 0)),
            pl.BlockSpec((ATTN_WIDTH, D_MODEL), lambda i: (1, 0)),
        ],
        out_specs=pl.BlockSpec((tm, D_MODEL), lambda i: (i, 0)),
        out_shape=jax.ShapeDtypeStruct((SEQ, D_MODEL), F32),
        scratch_shapes=[pltpu.VMEM((1, D_MODEL), F32)],
        compiler_params=_params(("parallel",), vmem),
        name="outproj",
    )(x, mod, g_post.reshape(1, D_MODEL), y_conv, y_attn, w_out, w_out)


def kernel(x, c, w_ada, b_ada, g_pre, g_post, w_ffn1_gate, w_ffn1_up, w_ffn1_down, w_in, b_in_conv, w_dw, b_dw, conv_ln_g, conv_ln_b, lam_q1, lam_k1, lam_q2, lam_k2, subln_g, w_out, w_ffn2_gate, w_ffn2_up, w_ffn2_down):
    bf = lambda w: w[0].astype(BF16)
    x2 = x.reshape(SEQ, D_MODEL)

    mod = _ada(c, w_ada[0], b_ada[0])

    x2 = _ffn(x2, mod, 0, g_pre[0, 0], g_post[0, 0], w_ffn1_gate[0], w_ffn1_up[0], w_ffn1_down[0])

    u_conv, qk, vt = _inproj(x2, mod, g_pre[0, 1], bf(w_in))
    y_conv = _conv(u_conv, b_in_conv[0], w_dw[0], b_dw[0], conv_ln_g[0], conv_ln_b[0])

    lam = (jnp.exp(jnp.sum(lam_q1[0].astype(F32) * lam_k1[0].astype(F32)))
           - jnp.exp(jnp.sum(lam_q2[0].astype(F32) * lam_k2[0].astype(F32)))
           + LAMBDA_INIT).reshape(1)
    slope_parts = jnp.asarray(_slope_parts())
    y_attn = _attention(qk, vt, slope_parts, lam, subln_g[0])

    x2 = _outproj(x2, mod, g_post[0, 1], y_conv, y_attn, bf(w_out))

    x2 = _ffn(x2, mod, 6, g_pre[0, 2], g_post[0, 2], w_ffn2_gate[0], w_ffn2_up[0], w_ffn2_down[0])
    return x2.reshape(1, SEQ, D_MODEL)
```

```python
import functools
import math

import jax
import jax.numpy as jnp
import numpy as np
from jax import lax
from jax.experimental import pallas as pl
from jax.experimental.pallas import tpu as pltpu

D_MODEL = 2048
SEQ = 8192
CHUNK = 64
CONV_WIDTH = D_MODEL // 2
ATTN_WIDTH = D_MODEL - CONV_WIDTH
CONV_KERNEL = 31
N_HEADS = 8
V_HEAD_DIM = ATTN_WIDTH // N_HEADS
QK_HEAD_DIM = V_HEAD_DIM // 2
D_FF = 5632
N_MOD = 9
IN_WIDTH = 2 * CONV_WIDTH + 3 * ATTN_WIDTH
EPS = 1e-6
NEG_INF = -1e30
LAMBDA_INIT = 0.8 - 0.6 * math.exp(-0.3 * 0)

F32 = jnp.float32
BF16 = jnp.bfloat16

V7X_VMEM_BUDGET_BYTES = 60 * 1024 * 1024

ADA_TN = 1024
FFN_TM = 1024
FFN_TF = 256
INPROJ_TM = 1024
OUTPROJ_TM = 512
PROJ_TN = 1024
CONV_TM = 256
CONV_HALO = 32
CONV_RC = 32
CONV_CC = 512
SUBLANES = 8
ATT_TQ = 1024
ATT_TK = 512
ATT_CB = 256
ATT_SKEW_S = 3
ATT_RING = 4


def _params(semantics, vmem_bytes, flags=None):
    return pltpu.CompilerParams(
        dimension_semantics=semantics,
        vmem_limit_bytes=min(int(vmem_bytes), V7X_VMEM_BUDGET_BYTES),
        flags=flags,
    )


ROW_CHUNK = 16


def _norm_rows(src_ref, scale_ref, dst_ref, finish):
    for c in range(src_ref.shape[0] // ROW_CHUNK):
        rows = slice(c * ROW_CHUNK, (c + 1) * ROW_CHUNK)
        x = src_ref[rows, :]
        ms = jnp.mean(x * x, axis=-1, keepdims=True)
        dst_ref[rows, :] = finish(rows, x * lax.rsqrt(ms + EPS) * scale_ref[...]).astype(dst_ref.dtype)


def _modulated_norm(x_ref, gpre_ref, scale_ref, shift_ref, h_ref, vec_ref):
    vec_ref[...] = gpre_ref[...] * (1.0 + scale_ref[...])
    _norm_rows(x_ref, vec_ref, h_ref, lambda rows, y: y + shift_ref[...])


def _gated_residual(x_ref, y_ref, gpost_ref, gate_ref, gate_factor, o_ref, vec_ref):
    vec_ref[...] = gpost_ref[...] * (gate_factor * gate_ref[...])
    _norm_rows(y_ref, vec_ref, o_ref, lambda rows, r: x_ref[rows, :] + r)


def _ada_kernel(c_ref, w_ref, b_ref, o_ref):
    c = c_ref[...]
    s = (c * jax.nn.sigmoid(c)).astype(BF16)
    s8 = jnp.broadcast_to(s, (8, D_MODEL))
    r = jnp.dot(s8, w_ref[...].astype(BF16), preferred_element_type=F32)
    o_ref[...] = r[0:1, :] + b_ref[...]


def _ada(c, w, b):
    n = w.shape[1]
    return pl.pallas_call(
        _ada_kernel,
        grid=(n // ADA_TN,),
        in_specs=[
            pl.BlockSpec((1, D_MODEL), lambda j: (0, 0)),
            pl.BlockSpec((D_MODEL, ADA_TN), lambda j: (0, j)),
            pl.BlockSpec((1, ADA_TN), lambda j: (0, j)),
        ],
        out_specs=pl.BlockSpec((1, ADA_TN), lambda j: (0, j)),
        out_shape=jax.ShapeDtypeStruct((1, n), F32),
        compiler_params=_params(("parallel",), 2 * D_MODEL * ADA_TN * 4 + D_MODEL * ADA_TN * 2 + (8 << 20)),
        name="adaln",
    )(c, w, b.reshape(1, n))


def _ffn_kernel(x_ref, shift_ref, scale_ref, gate_ref, gpre_ref, gpost_ref,
                wg_ref, wu_ref, wd_ref, o_ref, h_ref, vec_ref):
    j = pl.program_id(1)

    @pl.when(j == 0)
    def _():
        _modulated_norm(x_ref, gpre_ref, scale_ref, shift_ref, h_ref, vec_ref)
        o_ref[...] = jnp.zeros_like(o_ref)

    h = h_ref[...]
    g = jnp.dot(h, wg_ref[...].astype(BF16), preferred_element_type=F32)
    u = jnp.dot(h, wu_ref[...].astype(BF16), preferred_element_type=F32)
    a = (g * jax.nn.sigmoid(g) * u).astype(BF16)
    o_ref[...] += jnp.dot(a, wd_ref[...].astype(BF16), preferred_element_type=F32)

    @pl.when(j == pl.num_programs(1) - 1)
    def _():
        _gated_residual(x_ref, o_ref, gpost_ref, gate_ref, 0.5, o_ref, vec_ref)


def _ffn(x, mod, mod_base, g_pre, g_post, wg, wu, wd):
    tm, tf = FFN_TM, FFN_TF
    row = lambda k: pl.BlockSpec((1, D_MODEL), lambda i, j, k=k: (0, k))
    vec = pl.BlockSpec((1, D_MODEL), lambda i, j: (0, 0))
    vmem = (2 * 2 * tm * D_MODEL * 4
            + tm * D_MODEL * 2
            + 2 * 3 * D_MODEL * tf * 4
            + 3 * D_MODEL * tf * 2
            + 3 * tm * tf * 4
            + (4 << 20))
    return pl.pallas_call(
        _ffn_kernel,
        grid=(SEQ // tm, D_FF // tf),
        in_specs=[
            pl.BlockSpec((tm, D_MODEL), lambda i, j: (i, 0)),
            row(mod_base), row(mod_base + 1), row(mod_base + 2),
            vec, vec,
            pl.BlockSpec((D_MODEL, tf), lambda i, j: (0, j)),
            pl.BlockSpec((D_MODEL, tf), lambda i, j: (0, j)),
            pl.BlockSpec((tf, D_MODEL), lambda i, j: (j, 0)),
        ],
        out_specs=pl.BlockSpec((tm, D_MODEL), lambda i, j: (i, 0)),
        out_shape=jax.ShapeDtypeStruct((SEQ, D_MODEL), F32),
        scratch_shapes=[pltpu.VMEM((tm, D_MODEL), BF16), pltpu.VMEM((1, D_MODEL), F32)],
        compiler_params=_params(("parallel", "arbitrary"), vmem),
        name="ffn",
    )(x, mod, mod, mod, g_pre.reshape(1, D_MODEL), g_post.reshape(1, D_MODEL), wg, wu, wd)


N_U_BLOCKS = 2 * CONV_WIDTH // PROJ_TN
N_QK_BLOCKS = 2 * ATTN_WIDTH // PROJ_TN
assert PROJ_TN == ATTN_WIDTH
LOG2E = math.log2(math.e)
Q_SCALE_LOG2 = QK_HEAD_DIM ** -0.5 * LOG2E


def _inproj_kernel(x_ref, shift_ref, scale_ref, gpre_ref, w_ref, u_ref, qk_ref, vt_ref, h_ref, vec_ref):
    j = pl.program_id(1)

    @pl.when(j == 0)
    def _():
        _modulated_norm(x_ref, gpre_ref, scale_ref, shift_ref, h_ref, vec_ref)

    @pl.when(j < N_U_BLOCKS)
    def _():
        u_ref[...] = jnp.dot(h_ref[...], w_ref[...], preferred_element_type=F32)

    @pl.when(jnp.logical_and(j >= N_U_BLOCKS, j < N_U_BLOCKS + N_QK_BLOCKS))
    def _():
        c = jnp.where(j == N_U_BLOCKS, Q_SCALE_LOG2, 1.0)
        qk_ref[...] = (jnp.dot(h_ref[...], w_ref[...], preferred_element_type=F32) * c).astype(BF16)

    @pl.when(j == N_U_BLOCKS + N_QK_BLOCKS)
    def _():
        v = jnp.dot(h_ref[...], w_ref[...], preferred_element_type=F32)
        vt_ref[...] = v.T.astype(BF16)


def _inproj(x, mod, g_pre, w_in):
    tm, tn = INPROJ_TM, PROJ_TN
    n_main = N_U_BLOCKS + N_QK_BLOCKS
    vmem = (2 * tm * D_MODEL * 4 + tm * D_MODEL * 2 + 2 * D_MODEL * tn * 2
            + 2 * tm * tn * (4 + 2) + 2 * ATTN_WIDTH * tm * 2 + 2 * tm * tn * 4 + (4 << 20))
    return pl.pallas_call(
        _inproj_kernel,
        grid=(SEQ // tm, n_main + 1),
        in_specs=[
            pl.BlockSpec((tm, D_MODEL), lambda i, j: (i, 0)),
            pl.BlockSpec((1, D_MODEL), lambda i, j: (0, 3)),
            pl.BlockSpec((1, D_MODEL), lambda i, j: (0, 4)),
            pl.BlockSpec((1, D_MODEL), lambda i, j: (0, 0)),
            pl.BlockSpec((D_MODEL, tn), lambda i, j: (0, j)),
        ],
        out_specs=[
            pl.BlockSpec((tm, tn), lambda i, j: (i, jnp.minimum(j, N_U_BLOCKS - 1))),
            pl.BlockSpec((tm, tn), lambda i, j: (i, jnp.clip(j - N_U_BLOCKS, 0, N_QK_BLOCKS - 1))),
            pl.BlockSpec((ATTN_WIDTH, tm), lambda i, j: (0, i)),
        ],
        out_shape=[
            jax.ShapeDtypeStruct((SEQ, 2 * CONV_WIDTH), F32),
            jax.ShapeDtypeStruct((SEQ, 2 * ATTN_WIDTH), BF16),
            jax.ShapeDtypeStruct((ATTN_WIDTH, SEQ), BF16),
        ],
        scratch_shapes=[pltpu.VMEM((tm, D_MODEL), BF16), pltpu.VMEM((1, D_MODEL), F32)],
        compiler_params=_params(("parallel", "arbitrary"), vmem),
        name="inproj",
    )(x, mod, mod, g_pre.reshape(1, D_MODEL), w_in)


def _conv_kernel(u_ref, uprev_ref, bin_ref, wdw_ref, bdw_ref, lng_ref, lnb_ref, o_ref,
                 vbuf_ref, conv_ref, wb_ref):
    i = pl.program_id(0)
    b = bin_ref[...]

    def glu(u):
        u = u + b
        return u[:, :CONV_WIDTH] * jax.nn.sigmoid(u[:, CONV_WIDTH:])

    vbuf_ref[0, 0:CONV_HALO, :] = jnp.where(i > 0, glu(uprev_ref[...]), 0.0)
    vbuf_ref[0, CONV_HALO:, :] = glu(u_ref[...])
    n_shift = CONV_TM + CONV_HALO - SUBLANES
    for s in range(1, SUBLANES):
        vbuf_ref[s, 0:n_shift, :] = vbuf_ref[0, s:s + n_shift, :]

    for j in range(CONV_KERNEL):
        wb_ref[j] = jnp.broadcast_to(wdw_ref[j:j + 1, :], (SUBLANES, CONV_WIDTH))

    lead = CONV_HALO - (CONV_KERNEL - 1)
    for cb in range(CONV_WIDTH // CONV_CC):
        cs = slice(cb * CONV_CC, (cb + 1) * CONV_CC)
        for r in range(CONV_TM // CONV_RC):
            accs = [jnp.zeros((SUBLANES, CONV_CC), F32) for _ in range(CONV_RC // SUBLANES)]
            for j in range(CONV_KERNEL):
                s, lo = (lead + j) % SUBLANES, r * CONV_RC + (lead + j) // SUBLANES * SUBLANES
                w = wb_ref[j, :, cs]
                for q in range(len(accs)):
                    accs[q] = accs[q] + vbuf_ref[s, lo + q * SUBLANES:lo + (q + 1) * SUBLANES, cs] * w
            for q, acc in enumerate(accs):
                conv_ref[r * CONV_RC + q * SUBLANES:r * CONV_RC + (q + 1) * SUBLANES, cs] = acc

    v = conv_ref[...] + bdw_ref[...]
    mu = jnp.mean(v, axis=-1, keepdims=True)
    xc = v - mu
    var = jnp.mean(xc * xc, axis=-1, keepdims=True)
    y = xc * lax.rsqrt(var + EPS) * lng_ref[...] + lnb_ref[...]
    o_ref[...] = (y * jax.nn.sigmoid(y)).astype(BF16)


def _conv(u, b_in, w_dw, b_dw, ln_g, ln_b):
    tm = CONV_TM
    halo_blocks = tm // CONV_HALO
    vec = lambda n: pl.BlockSpec((1, n), lambda i: (0, 0))
    vmem = (2 * tm * 2 * CONV_WIDTH * 4 + 2 * CONV_HALO * 2 * CONV_WIDTH * 4
            + (SUBLANES * (tm + CONV_HALO) + tm) * CONV_WIDTH * 4 + 2 * tm * CONV_WIDTH * 2
            + 6 * tm * CONV_WIDTH * 4 + (6 << 20))
    return pl.pallas_call(
        _conv_kernel,
        grid=(SEQ // tm,),
        in_specs=[
            pl.BlockSpec((tm, 2 * CONV_WIDTH), lambda i: (i, 0)),
            pl.BlockSpec((CONV_HALO, 2 * CONV_WIDTH), lambda i: (jnp.maximum(i * halo_blocks - 1, 0), 0)),
            vec(2 * CONV_WIDTH),
            pl.BlockSpec((CONV_KERNEL, CONV_WIDTH), lambda i: (0, 0)),
            vec(CONV_WIDTH), vec(CONV_WIDTH), vec(CONV_WIDTH),
        ],
        out_specs=pl.BlockSpec((tm, CONV_WIDTH), lambda i: (i, 0)),
        out_shape=jax.ShapeDtypeStruct((SEQ, CONV_WIDTH), BF16),
        scratch_shapes=[pltpu.VMEM((SUBLANES, tm + CONV_HALO, CONV_WIDTH), F32),
                        pltpu.VMEM((tm, CONV_WIDTH), F32),
                        pltpu.VMEM((CONV_KERNEL, SUBLANES, CONV_WIDTH), F32)],
        compiler_params=_params(("parallel",), vmem),
        name="convbranch",
    )(u, u, b_in.reshape(1, -1), w_dw, b_dw.reshape(1, -1), ln_g.reshape(1, -1), ln_b.reshape(1, -1))


N_SLOPE_PARTS = 3
AUG_LANES = 128
POS_SPLIT = 256
assert POS_SPLIT <= 256 and ATT_TQ // POS_SPLIT <= 256
assert ATT_TQ % ATT_TK == 0 and ATT_TK % ATT_CB == 0 and ATT_CB % CHUNK == 0


def _aug_operand(n_rows, parts, key_side):
    lane = lax.broadcasted_iota(jnp.int32, (n_rows, AUG_LANES), 1)
    r = lax.broadcasted_iota(jnp.int32, (n_rows, AUG_LANES), 0)
    hi = ((r // POS_SPLIT) * POS_SPLIT).astype(F32)
    lo = (r % POS_SPLIT).astype(F32)
    piece = lane % N_SLOPE_PARTS
    part = jnp.where(piece == 0, parts[0], jnp.where(piece == 1, parts[1], parts[2]))
    zero = jnp.zeros((n_rows, AUG_LANES), F32)
    if key_side:
        vals = (-part, -part, hi, lo)
    else:
        vals = (hi, lo, part, part)
    out = zero
    for g, v in enumerate(vals):
        out = jnp.where(lane // N_SLOPE_PARTS == g, v, out)
    return out


def _diagonal_blocks():
    tq, tk, cb = ATT_TQ, ATT_TK, ATT_CB
    nqc = tq // cb
    out = []
    for d in range(tq // tk):
        for b in range(2 * nqc):
            q_lo = (b % nqc) * cb
            if q_lo + cb <= d * tk:
                continue
            out.append((d, b, None if q_lo >= (d + 1) * tk else 1 + (q_lo - d * tk) // cb))
    return out


def _attn_kernel(parts_ref, lam_ref, q_ref, k_ref, vt_ref, g_ref, o_ref,
                 qs_ref, kaug_ref, corr_ref, s_ref, m_ref, l_ref, acc_ref):
    tq, tk, cb = ATT_TQ, ATT_TK, ATT_CB
    nqc = tq // cb
    nb = 2 * nqc
    h = pl.program_id(0)
    i = pl.program_id(1)
    n_before = i * (tq // tk)
    parts = [parts_ref[h * N_SLOPE_PARTS + n] for n in range(N_SLOPE_PARTS)]
    sl = parts[0] + parts[1] + parts[2]

    @pl.when(i == 0)
    def _():
        kaug_ref[...] = _aug_operand(tk, parts, key_side=True).astype(BF16)
        qaug = _aug_operand(tq, parts, key_side=False).astype(BF16)
        qs_ref[0:tq, V_HEAD_DIM:] = qaug
        qs_ref[tq:, V_HEAD_DIM:] = qaug
        corr_ref[0] = jnp.zeros((tk, cb), F32)
        for c in range(tk // cb):
            krel = lax.broadcasted_iota(jnp.int32, (tk, cb), 0)
            qrel = lax.broadcasted_iota(jnp.int32, (tk, cb), 1) + c * cb
            after = jnp.maximum(krel - qrel, 0).astype(F32)
            allowed = (krel // CHUNK) <= (qrel // CHUNK)
            corr_ref[1 + c] = jnp.where(allowed, -2.0 * sl * after, NEG_INF)

    q = q_ref[...]
    lane = lax.broadcasted_iota(jnp.int32, (tq, V_HEAD_DIM), 1)
    zero = jnp.zeros_like(q)
    qs_ref[0:tq, 0:V_HEAD_DIM] = jnp.where(lane < QK_HEAD_DIM, q, zero)
    qs_ref[tq:, 0:V_HEAD_DIM] = jnp.where(lane >= QK_HEAD_DIM, q, zero)

    m_ref[...] = jnp.full_like(m_ref, NEG_INF)
    l_ref[...] = jnp.zeros_like(l_ref)
    acc_ref[...] = jnp.zeros_like(acc_ref)

    def scores(t, b, slot):
        k = k_ref[pl.ds(pl.multiple_of(t * tk, tk), tk), :]
        kx = jnp.concatenate([k, kaug_ref[...]], axis=1)
        s_ref[slot] = lax.dot_general(kx, qs_ref[b * cb:(b + 1) * cb, :], (((1,), (1,)), ((), ())),
                                      preferred_element_type=F32)

    def value_tile(t):
        return vt_ref[:, pl.ds(pl.multiple_of(t * tk, tk), tk)]

    def softmax_values(t, b, slot, corr, vt):
        off = sl * (i * tq - t * tk).astype(F32)
        s = s_ref[slot]
        if corr is not None:
            s = s + corr_ref[corr]
        m_prev = m_ref[b]
        m_new = jnp.maximum(m_prev, jnp.max(s, axis=0, keepdims=True) - off)
        alpha = jnp.exp2(m_prev - m_new)
        p = jnp.exp2(s - (m_new + off))
        l_ref[b] = alpha * l_ref[b] + jnp.sum(p, axis=0, keepdims=True)
        m_ref[b] = m_new
        pv = jnp.dot(vt, p.astype(BF16), preferred_element_type=F32)
        acc_ref[b] = alpha * acc_ref[b] + pv

    blocks = _diagonal_blocks()
    assert [blk[:2] for blk in blocks[:ATT_SKEW_S]] == [(0, b) for b in range(ATT_SKEW_S)]
    assert nb % ATT_RING == 0

    for b in range(ATT_SKEW_S):
        scores(0, b, b % ATT_RING)

    tiles_per_iter = tq // tk

    def body(jj, carry):
        for u in range(tiles_per_iter):
            j = jj * tiles_per_iter + u
            vt = value_tile(j)
            for b in range(nb):
                bs = b + ATT_SKEW_S
                scores(j + bs // nb, bs % nb, bs % ATT_RING)
                softmax_values(j, b, b % ATT_RING, None, vt)
        return carry

    lax.fori_loop(0, i, body, 0)

    vts = {}
    for n, (d, b, corr) in enumerate(blocks):
        if d not in vts:
            vts[d] = value_tile(n_before + d)
        if n + ATT_SKEW_S < len(blocks):
            d2, b2, _ = blocks[n + ATT_SKEW_S]
            scores(n_before + d2, b2, (n + ATT_SKEW_S) % ATT_RING)
        softmax_values(n_before + d, b, n % ATT_RING, corr, vts[d])

    lam = lam_ref[0]
    for c in range(nqc):
        d = acc_ref[c] / l_ref[c] - lam * (acc_ref[nqc + c] / l_ref[nqc + c])
        ms = jnp.mean(d * d, axis=0, keepdims=True)
        y = d * lax.rsqrt(ms + EPS) * g_ref[...] * (1.0 - LAMBDA_INIT)
        o_ref[c * cb:(c + 1) * cb, :] = y.T.astype(BF16)


def _slope_parts():
    rest = (2.0 ** (-8.0 * (np.arange(N_HEADS, dtype=np.float64) + 1.0) / N_HEADS) * LOG2E).astype(np.float32)
    pieces = []
    for _ in range(N_SLOPE_PARTS):
        piece = rest.astype(jnp.bfloat16).astype(np.float32)
        pieces.append(piece)
        rest = rest - piece
    return np.stack(pieces, axis=1).reshape(-1)


def _attention(qk, vt, slope_parts, lam, subln_g):
    tq, tk, cb = ATT_TQ, ATT_TK, ATT_CB
    nh = N_HEADS
    nb = 2 * tq // cb
    smem = pl.BlockSpec(memory_space=pltpu.SMEM)
    scratch = [
        pltpu.VMEM((2 * tq, V_HEAD_DIM + AUG_LANES), BF16),
        pltpu.VMEM((tk, AUG_LANES), BF16),
        pltpu.VMEM((1 + tk // cb, tk, cb), F32),
        pltpu.VMEM((ATT_RING, tk, cb), F32),
        pltpu.VMEM((nb, 1, cb), F32),
        pltpu.VMEM((nb, 1, cb), F32),
        pltpu.VMEM((nb, V_HEAD_DIM, cb), F32),
    ]
    vmem = (2 * 2 * SEQ * V_HEAD_DIM * 2 + 4 * tq * V_HEAD_DIM * 2
            + 2 * tq * (V_HEAD_DIM + AUG_LANES) * 2 + tk * AUG_LANES * 2 + (1 + tk // cb) * tk * cb * 4
            + ATT_RING * tk * cb * 4 + 2 * nb * 8 * cb * 4
            + nb * V_HEAD_DIM * cb * 4 + 8 * tk * cb * 4 + (6 << 20))
    return pl.pallas_call(
        _attn_kernel,
        grid=(nh, SEQ // tq),
        in_specs=[
            smem, smem,
            pl.BlockSpec((tq, V_HEAD_DIM), lambda h, i: (i, h)),
            pl.BlockSpec((SEQ, V_HEAD_DIM), lambda h, i: (0, nh + h)),
            pl.BlockSpec((V_HEAD_DIM, SEQ), lambda h, i: (h, 0)),
            pl.BlockSpec((V_HEAD_DIM, 1), lambda h, i: (0, 0)),
        ],
        out_specs=pl.BlockSpec((tq, V_HEAD_DIM), lambda h, i: (i, h)),
        out_shape=jax.ShapeDtypeStruct((SEQ, ATTN_WIDTH), BF16),
        scratch_shapes=scratch,
        compiler_params=_params(("parallel", "arbitrary"), vmem),
        name="diffattn",
    )(slope_parts, lam, qk, qk, vt, subln_g.reshape(V_HEAD_DIM, 1))


def _outproj_kernel(x_ref, gate_ref, gpost_ref, yc_ref, ya_ref, wc_ref, wa_ref, o_ref, vec_ref):
    y = jnp.dot(yc_ref[...], wc_ref[...], preferred_element_type=F32)
    o_ref[...] = y + jnp.dot(ya_ref[...], wa_ref[...], preferred_element_type=F32)
    _gated_residual(x_ref, o_ref, gpost_ref, gate_ref, 1.0, o_ref, vec_ref)


def _outproj(x, mod, g_post, y_conv, y_attn, w_out):
    tm = OUTPROJ_TM
    vmem = (2 * 2 * tm * D_MODEL * 4 + 2 * 2 * tm * CONV_WIDTH * 2 + 2 * D_MODEL * D_MODEL * 2
            + 3 * tm * D_MODEL * 4 + (6 << 20))
    return pl.pallas_call(
        _outproj_kernel,
        grid=(SEQ // tm,),
        in_specs=[
            pl.BlockSpec((tm, D_MODEL), lambda i: (i, 0)),
            pl.BlockSpec((1, D_MODEL), lambda i: (0, 5)),
            pl.BlockSpec((1, D_MODEL), lambda i: (0, 0)),
            pl.BlockSpec((tm, CONV_WIDTH), lambda i: (i, 0)),
            pl.BlockSpec((tm, ATTN_WIDTH), lambda i: (i, 0)),
            pl.BlockSpec((CONV_WIDTH, D_MODEL), lambda i: (0, 0)),
            pl.BlockSpec((ATTN_WIDTH, D_MODEL), lambda i: (1, 0)),
        ],
        out_specs=pl.BlockSpec((tm, D_MODEL), lambda i: (i, 0)),
        out_shape=jax.ShapeDtypeStruct((SEQ, D_MODEL), F32),
        scratch_shapes=[pltpu.VMEM((1, D_MODEL), F32)],
        compiler_params=_params(("parallel",), vmem),
        name="outproj",
    )(x, mod, g_post.reshape(1, D_MODEL), y_conv, y_attn, w_out, w_out)


def kernel(x, c, w_ada, b_ada, g_pre, g_post, w_ffn1_gate, w_ffn1_up, w_ffn1_down, w_in, b_in_conv, w_dw, b_dw, conv_ln_g, conv_ln_b, lam_q1, lam_k1, lam_q2, lam_k2, subln_g, w_out, w_ffn2_gate, w_ffn2_up, w_ffn2_down):
    bf = lambda w: w[0].astype(BF16)
    x2 = x.reshape(SEQ, D_MODEL)

    mod = _ada(c, w_ada[0], b_ada[0])

    x2 = _ffn(x2, mod, 0, g_pre[0, 0], g_post[0, 0], w_ffn1_gate[0], w_ffn1_up[0], w_ffn1_down[0])

    u_conv, qk, vt = _inproj(x2, mod, g_pre[0, 1], bf(w_in))
    y_conv = _conv(u_conv, b_in_conv[0], w_dw[0], b_dw[0], conv_ln_g[0], conv_ln_b[0])

    lam = (jnp.exp(jnp.sum(lam_q1[0].astype(F32) * lam_k1[0].astype(F32)))
           - jnp.exp(jnp.sum(lam_q2[0].astype(F32) * lam_k2[0].astype(F32)))
           + LAMBDA_INIT).reshape(1)
    slope_parts = jnp.asarray(_slope_parts())
    y_attn = _attention(qk, vt, slope_parts, lam, subln_g[0])

    x2 = _outproj(x2, mod, g_post[0, 1], y_conv, y_attn, bf(w_out))

    x2 = _ffn(x2, mod, 6, g_pre[0, 2], g_post[0, 2], w_ffn2_gate[0], w_ffn2_up[0], w_ffn2_down[0])
    return x2.reshape(1, SEQ, D_MODEL)
```

```python
import functools
import math

import jax
import jax.numpy as jnp
import numpy as np
from jax import lax
from jax.experimental import pallas as pl
from jax.experimental.pallas import tpu as pltpu

D_MODEL = 2048
SEQ = 8192
CHUNK = 64
CONV_WIDTH = D_MODEL // 2
ATTN_WIDTH = D_MODEL - CONV_WIDTH
CONV_KERNEL = 31
N_HEADS = 8
V_HEAD_DIM = ATTN_WIDTH // N_HEADS
QK_HEAD_DIM = V_HEAD_DIM // 2
D_FF = 5632
N_MOD = 9
IN_WIDTH = 2 * CONV_WIDTH + 3 * ATTN_WIDTH
EPS = 1e-6
NEG_INF = -1e30
LAMBDA_INIT = 0.8 - 0.6 * math.exp(-0.3 * 0)

F32 = jnp.float32
BF16 = jnp.bfloat16

V7X_VMEM_BUDGET_BYTES = 60 * 1024 * 1024

ADA_TN = 1024
FFN_TM = 1024
FFN_TF = 256
INPROJ_TM = 1024
OUTPROJ_TM = 512
PROJ_TN = 1024
CONV_TM = 256
CONV_HALO = 32
CONV_RC = 32
CONV_CC = 512
SUBLANES = 8
ATT_TQ = 2048
ATT_TK = 512
ATT_CB = 256
ATT_SKEW_S = 3
ATT_RING = 4


def _params(semantics, vmem_bytes, flags=None):
    return pltpu.CompilerParams(
        dimension_semantics=semantics,
        vmem_limit_bytes=min(int(vmem_bytes), V7X_VMEM_BUDGET_BYTES),
        flags=flags,
    )


ROW_CHUNK = 16


def _norm_rows(src_ref, scale_ref, dst_ref, finish):
    for c in range(src_ref.shape[0] // ROW_CHUNK):
        rows = slice(c * ROW_CHUNK, (c + 1) * ROW_CHUNK)
        x = src_ref[rows, :]
        ms = jnp.mean(x * x, axis=-1, keepdims=True)
        dst_ref[rows, :] = finish(rows, x * lax.rsqrt(ms + EPS) * scale_ref[...]).astype(dst_ref.dtype)


def _modulated_norm(x_ref, gpre_ref, scale_ref, shift_ref, h_ref, vec_ref):
    vec_ref[...] = gpre_ref[...] * (1.0 + scale_ref[...])
    _norm_rows(x_ref, vec_ref, h_ref, lambda rows, y: y + shift_ref[...])


def _gated_residual(x_ref, y_ref, gpost_ref, gate_ref, gate_factor, o_ref, vec_ref):
    vec_ref[...] = gpost_ref[...] * (gate_factor * gate_ref[...])
    _norm_rows(y_ref, vec_ref, o_ref, lambda rows, r: x_ref[rows, :] + r)


def _ada_kernel(c_ref, w_ref, b_ref, o_ref):
    c = c_ref[...]
    s = (c * jax.nn.sigmoid(c)).astype(BF16)
    s8 = jnp.broadcast_to(s, (8, D_MODEL))
    r = jnp.dot(s8, w_ref[...].astype(BF16), preferred_element_type=F32)
    o_ref[...] = r[0:1, :] + b_ref[...]


def _ada(c, w, b):
    n = w.shape[1]
    return pl.pallas_call(
        _ada_kernel,
        grid=(n // ADA_TN,),
        in_specs=[
            pl.BlockSpec((1, D_MODEL), lambda j: (0, 0)),
            pl.BlockSpec((D_MODEL, ADA_TN), lambda j: (0, j)),
            pl.BlockSpec((1, ADA_TN), lambda j: (0, j)),
        ],
        out_specs=pl.BlockSpec((1, ADA_TN), lambda j: (0, j)),
        out_shape=jax.ShapeDtypeStruct((1, n), F32),
        compiler_params=_params(("parallel",), 2 * D_MODEL * ADA_TN * 4 + D_MODEL * ADA_TN * 2 + (8 << 20)),
        name="adaln",
    )(c, w, b.reshape(1, n))


def _ffn_kernel(x_ref, shift_ref, scale_ref, gate_ref, gpre_ref, gpost_ref,
                wg_ref, wu_ref, wd_ref, o_ref, h_ref, vec_ref):
    j = pl.program_id(1)

    @pl.when(j == 0)
    def _():
        _modulated_norm(x_ref, gpre_ref, scale_ref, shift_ref, h_ref, vec_ref)
        o_ref[...] = jnp.zeros_like(o_ref)

    h = h_ref[...]
    g = jnp.dot(h, wg_ref[...].astype(BF16), preferred_element_type=F32)
    u = jnp.dot(h, wu_ref[...].astype(BF16), preferred_element_type=F32)
    a = (g * jax.nn.sigmoid(g) * u).astype(BF16)
    o_ref[...] += jnp.dot(a, wd_ref[...].astype(BF16), preferred_element_type=F32)

    @pl.when(j == pl.num_programs(1) - 1)
    def _():
        _gated_residual(x_ref, o_ref, gpost_ref, gate_ref, 0.5, o_ref, vec_ref)


def _ffn(x, mod, mod_base, g_pre, g_post, wg, wu, wd):
    tm, tf = FFN_TM, FFN_TF
    row = lambda k: pl.BlockSpec((1, D_MODEL), lambda i, j, k=k: (0, k))
    vec = pl.BlockSpec((1, D_MODEL), lambda i, j: (0, 0))
    vmem = (2 * 2 * tm * D_MODEL * 4
            + tm * D_MODEL * 2
            + 2 * 3 * D_MODEL * tf * 4
            + 3 * D_MODEL * tf * 2
            + 3 * tm * tf * 4
            + (4 << 20))
    return pl.pallas_call(
        _ffn_kernel,
        grid=(SEQ // tm, D_FF // tf),
        in_specs=[
            pl.BlockSpec((tm, D_MODEL), lambda i, j: (i, 0)),
            row(mod_base), row(mod_base + 1), row(mod_base + 2),
            vec, vec,
            pl.BlockSpec((D_MODEL, tf), lambda i, j: (0, j)),
            pl.BlockSpec((D_MODEL, tf), lambda i, j: (0, j)),
            pl.BlockSpec((tf, D_MODEL), lambda i, j: (j, 0)),
        ],
        out_specs=pl.BlockSpec((tm, D_MODEL), lambda i, j: (i, 0)),
        out_shape=jax.ShapeDtypeStruct((SEQ, D_MODEL), F32),
        scratch_shapes=[pltpu.VMEM((tm, D_MODEL), BF16), pltpu.VMEM((1, D_MODEL), F32)],
        compiler_params=_params(("parallel", "arbitrary"), vmem),
        name="ffn",
    )(x, mod, mod, mod, g_pre.reshape(1, D_MODEL), g_post.reshape(1, D_MODEL), wg, wu, wd)


N_U_BLOCKS = 2 * CONV_WIDTH // PROJ_TN
N_QK_BLOCKS = 2 * ATTN_WIDTH // PROJ_TN
assert PROJ_TN == ATTN_WIDTH
LOG2E = math.log2(math.e)
Q_SCALE_LOG2 = QK_HEAD_DIM ** -0.5 * LOG2E


def _inproj_kernel(x_ref, shift_ref, scale_ref, gpre_ref, w_ref, u_ref, qk_ref, vt_ref, h_ref, vec_ref):
    j = pl.program_id(1)

    @pl.when(j == 0)
    def _():
        _modulated_norm(x_ref, gpre_ref, scale_ref, shift_ref, h_ref, vec_ref)

    @pl.when(j < N_U_BLOCKS)
    def _():
        u_ref[...] = jnp.dot(h_ref[...], w_ref[...], preferred_element_type=F32)

    @pl.when(jnp.logical_and(j >= N_U_BLOCKS, j < N_U_BLOCKS + N_QK_BLOCKS))
    def _():
        c = jnp.where(j == N_U_BLOCKS, Q_SCALE_LOG2, 1.0)
        qk_ref[...] = (jnp.dot(h_ref[...], w_ref[...], preferred_element_type=F32) * c).astype(BF16)

    @pl.when(j == N_U_BLOCKS + N_QK_BLOCKS)
    def _():
        v = jnp.dot(h_ref[...], w_ref[...], preferred_element_type=F32)
        vt_ref[...] = v.T.astype(BF16)


def _inproj(x, mod, g_pre, w_in):
    tm, tn = INPROJ_TM, PROJ_TN
    n_main = N_U_BLOCKS + N_QK_BLOCKS
    vmem = (2 * tm * D_MODEL * 4 + tm * D_MODEL * 2 + 2 * D_MODEL * tn * 2
            + 2 * tm * tn * (4 + 2) + 2 * ATTN_WIDTH * tm * 2 + 2 * tm * tn * 4 + (4 << 20))
    return pl.pallas_call(
        _inproj_kernel,
        grid=(SEQ // tm, n_main + 1),
        in_specs=[
            pl.BlockSpec((tm, D_MODEL), lambda i, j: (i, 0)),
            pl.BlockSpec((1, D_MODEL), lambda i, j: (0, 3)),
            pl.BlockSpec((1, D_MODEL), lambda i, j: (0, 4)),
            pl.BlockSpec((1, D_MODEL), lambda i, j: (0, 0)),
            pl.BlockSpec((D_MODEL, tn), lambda i, j: (0, j)),
        ],
        out_specs=[
            pl.BlockSpec((tm, tn), lambda i, j: (i, jnp.minimum(j, N_U_BLOCKS - 1))),
            pl.BlockSpec((tm, tn), lambda i, j: (i, jnp.clip(j - N_U_BLOCKS, 0, N_QK_BLOCKS - 1))),
            pl.BlockSpec((ATTN_WIDTH, tm), lambda i, j: (0, i)),
        ],
        out_shape=[
            jax.ShapeDtypeStruct((SEQ, 2 * CONV_WIDTH), F32),
            jax.ShapeDtypeStruct((SEQ, 2 * ATTN_WIDTH), BF16),
            jax.ShapeDtypeStruct((ATTN_WIDTH, SEQ), BF16),
        ],
        scratch_shapes=[pltpu.VMEM((tm, D_MODEL), BF16), pltpu.VMEM((1, D_MODEL), F32)],
        compiler_params=_params(("parallel", "arbitrary"), vmem),
        name="inproj",
    )(x, mod, mod, g_pre.reshape(1, D_MODEL), w_in)


def _conv_kernel(u_ref, uprev_ref, bin_ref, wdw_ref, bdw_ref, lng_ref, lnb_ref, o_ref,
                 vbuf_ref, conv_ref, wb_ref):
    i = pl.program_id(0)
    b = bin_ref[...]

    def glu(u):
        u = u + b
        return u[:, :CONV_WIDTH] * jax.nn.sigmoid(u[:, CONV_WIDTH:])

    vbuf_ref[0, 0:CONV_HALO, :] = jnp.where(i > 0, glu(uprev_ref[...]), 0.0)
    vbuf_ref[0, CONV_HALO:, :] = glu(u_ref[...])
    n_shift = CONV_TM + CONV_HALO - SUBLANES
    for s in range(1, SUBLANES):
        vbuf_ref[s, 0:n_shift, :] = vbuf_ref[0, s:s + n_shift, :]

    for j in range(CONV_KERNEL):
        wb_ref[j] = jnp.broadcast_to(wdw_ref[j:j + 1, :], (SUBLANES, CONV_WIDTH))

    lead = CONV_HALO - (CONV_KERNEL - 1)
    for cb in range(CONV_WIDTH // CONV_CC):
        cs = slice(cb * CONV_CC, (cb + 1) * CONV_CC)
        for r in range(CONV_TM // CONV_RC):
            accs = [jnp.zeros((SUBLANES, CONV_CC), F32) for _ in range(CONV_RC // SUBLANES)]
            for j in range(CONV_KERNEL):
                s, lo = (lead + j) % SUBLANES, r * CONV_RC + (lead + j) // SUBLANES * SUBLANES
                w = wb_ref[j, :, cs]
                for q in range(len(accs)):
                    accs[q] = accs[q] + vbuf_ref[s, lo + q * SUBLANES:lo + (q + 1) * SUBLANES, cs] * w
            for q, acc in enumerate(accs):
                conv_ref[r * CONV_RC + q * SUBLANES:r * CONV_RC + (q + 1) * SUBLANES, cs] = acc

    v = conv_ref[...] + bdw_ref[...]
    mu = jnp.mean(v, axis=-1, keepdims=True)
    xc = v - mu
    var = jnp.mean(xc * xc, axis=-1, keepdims=True)
    y = xc * lax.rsqrt(var + EPS) * lng_ref[...] + lnb_ref[...]
    o_ref[...] = (y * jax.nn.sigmoid(y)).astype(BF16)


def _conv(u, b_in, w_dw, b_dw, ln_g, ln_b):
    tm = CONV_TM
    halo_blocks = tm // CONV_HALO
    vec = lambda n: pl.BlockSpec((1, n), lambda i: (0, 0))
    vmem = (2 * tm * 2 * CONV_WIDTH * 4 + 2 * CONV_HALO * 2 * CONV_WIDTH * 4
            + (SUBLANES * (tm + CONV_HALO) + tm) * CONV_WIDTH * 4 + 2 * tm * CONV_WIDTH * 2
            + 6 * tm * CONV_WIDTH * 4 + (6 << 20))
    return pl.pallas_call(
        _conv_kernel,
        grid=(SEQ // tm,),
        in_specs=[
            pl.BlockSpec((tm, 2 * CONV_WIDTH), lambda i: (i, 0)),
            pl.BlockSpec((CONV_HALO, 2 * CONV_WIDTH), lambda i: (jnp.maximum(i * halo_blocks - 1, 0), 0)),
            vec(2 * CONV_WIDTH),
            pl.BlockSpec((CONV_KERNEL, CONV_WIDTH), lambda i: (0, 0)),
            vec(CONV_WIDTH), vec(CONV_WIDTH), vec(CONV_WIDTH),
        ],
        out_specs=pl.BlockSpec((tm, CONV_WIDTH), lambda i: (i, 0)),
        out_shape=jax.ShapeDtypeStruct((SEQ, CONV_WIDTH), BF16),
        scratch_shapes=[pltpu.VMEM((SUBLANES, tm + CONV_HALO, CONV_WIDTH), F32),
                        pltpu.VMEM((tm, CONV_WIDTH), F32),
                        pltpu.VMEM((CONV_KERNEL, SUBLANES, CONV_WIDTH), F32)],
        compiler_params=_params(("parallel",), vmem),
        name="convbranch",
    )(u, u, b_in.reshape(1, -1), w_dw, b_dw.reshape(1, -1), ln_g.reshape(1, -1), ln_b.reshape(1, -1))


N_SLOPE_PARTS = 3
AUG_LANES = 128
POS_SPLIT = 256
assert POS_SPLIT <= 256 and ATT_TQ // POS_SPLIT <= 256
assert ATT_TQ % ATT_TK == 0 and ATT_TK % ATT_CB == 0 and ATT_CB % CHUNK == 0


def _aug_operand(n_rows, parts, key_side):
    lane = lax.broadcasted_iota(jnp.int32, (n_rows, AUG_LANES), 1)
    r = lax.broadcasted_iota(jnp.int32, (n_rows, AUG_LANES), 0)
    hi = ((r // POS_SPLIT) * POS_SPLIT).astype(F32)
    lo = (r % POS_SPLIT).astype(F32)
    piece = lane % N_SLOPE_PARTS
    part = jnp.where(piece == 0, parts[0], jnp.where(piece == 1, parts[1], parts[2]))
    zero = jnp.zeros((n_rows, AUG_LANES), F32)
    if key_side:
        vals = (-part, -part, hi, lo)
    else:
        vals = (hi, lo, part, part)
    out = zero
    for g, v in enumerate(vals):
        out = jnp.where(lane // N_SLOPE_PARTS == g, v, out)
    return out


def _diagonal_blocks():
    tq, tk, cb = ATT_TQ, ATT_TK, ATT_CB
    nqc = tq // cb
    out = []
    for d in range(tq // tk):
        for b in range(2 * nqc):
            q_lo = (b % nqc) * cb
            if q_lo + cb <= d * tk:
                continue
            out.append((d, b, None if q_lo >= (d + 1) * tk else 1 + (q_lo - d * tk) // cb))
    return out


def _attn_kernel(parts_ref, lam_ref, q_ref, k_ref, vt_ref, g_ref, o_ref,
                 qs_ref, kaug_ref, corr_ref, s_ref, m_ref, l_ref, acc_ref):
    tq, tk, cb = ATT_TQ, ATT_TK, ATT_CB
    nqc = tq // cb
    nb = 2 * nqc
    h = pl.program_id(0)
    i = pl.program_id(1)
    n_before = i * (tq // tk)
    parts = [parts_ref[h * N_SLOPE_PARTS + n] for n in range(N_SLOPE_PARTS)]
    sl = parts[0] + parts[1] + parts[2]

    @pl.when(i == 0)
    def _():
        kaug_ref[...] = _aug_operand(tk, parts, key_side=True).astype(BF16)
        qaug = _aug_operand(tq, parts, key_side=False).astype(BF16)
        qs_ref[0:tq, V_HEAD_DIM:] = qaug
        qs_ref[tq:, V_HEAD_DIM:] = qaug
        corr_ref[0] = jnp.zeros((tk, cb), F32)
        for c in range(tk // cb):
            krel = lax.broadcasted_iota(jnp.int32, (tk, cb), 0)
            qrel = lax.broadcasted_iota(jnp.int32, (tk, cb), 1) + c * cb
            after = jnp.maximum(krel - qrel, 0).astype(F32)
            allowed = (krel // CHUNK) <= (qrel // CHUNK)
            corr_ref[1 + c] = jnp.where(allowed, -2.0 * sl * after, NEG_INF)

    q = q_ref[...]
    lane = lax.broadcasted_iota(jnp.int32, (tq, V_HEAD_DIM), 1)
    zero = jnp.zeros_like(q)
    qs_ref[0:tq, 0:V_HEAD_DIM] = jnp.where(lane < QK_HEAD_DIM, q, zero)
    qs_ref[tq:, 0:V_HEAD_DIM] = jnp.where(lane >= QK_HEAD_DIM, q, zero)

    m_ref[...] = jnp.full_like(m_ref, NEG_INF)
    l_ref[...] = jnp.zeros_like(l_ref)
    acc_ref[...] = jnp.zeros_like(acc_ref)

    def scores(t, b, slot):
        k = k_ref[pl.ds(pl.multiple_of(t * tk, tk), tk), :]
        kx = jnp.concatenate([k, kaug_ref[...]], axis=1)
        s_ref[slot] = lax.dot_general(kx, qs_ref[b * cb:(b + 1) * cb, :], (((1,), (1,)), ((), ())),
                                      preferred_element_type=F32)

    def value_tile(t):
        return vt_ref[:, pl.ds(pl.multiple_of(t * tk, tk), tk)]

    def softmax_values(t, b, slot, corr, vt):
        off = sl * (i * tq - t * tk).astype(F32)
        s = s_ref[slot]
        if corr is not None:
            s = s + corr_ref[corr]
        m_prev = m_ref[b]
        m_new = jnp.maximum(m_prev, jnp.max(s, axis=0, keepdims=True) - off)
        alpha = jnp.exp2(m_prev - m_new)
        p = jnp.exp2(s - (m_new + off))
        l_ref[b] = alpha * l_ref[b] + jnp.sum(p, axis=0, keepdims=True)
        m_ref[b] = m_new
        pv = jnp.dot(vt, p.astype(BF16), preferred_element_type=F32)
        acc_ref[b] = alpha * acc_ref[b] + pv

    blocks = _diagonal_blocks()
    assert [blk[:2] for blk in blocks[:ATT_SKEW_S]] == [(0, b) for b in range(ATT_SKEW_S)]
    assert nb % ATT_RING == 0

    for b in range(ATT_SKEW_S):
        scores(0, b, b % ATT_RING)

    tiles_per_iter = tq // tk

    def body(jj, carry):
        for u in range(tiles_per_iter):
            j = jj * tiles_per_iter + u
            vt = value_tile(j)
            for b in range(nb):
                bs = b + ATT_SKEW_S
                scores(j + bs // nb, bs % nb, bs % ATT_RING)
                softmax_values(j, b, b % ATT_RING, None, vt)
        return carry

    lax.fori_loop(0, i, body, 0)

    vts = {}
    for n, (d, b, corr) in enumerate(blocks):
        if d not in vts:
            vts[d] = value_tile(n_before + d)
        if n + ATT_SKEW_S < len(blocks):
            d2, b2, _ = blocks[n + ATT_SKEW_S]
            scores(n_before + d2, b2, (n + ATT_SKEW_S) % ATT_RING)
        softmax_values(n_before + d, b, n % ATT_RING, corr, vts[d])

    lam = lam_ref[0]
    for c in range(nqc):
        d = acc_ref[c] / l_ref[c] - lam * (acc_ref[nqc + c] / l_ref[nqc + c])
        ms = jnp.mean(d * d, axis=0, keepdims=True)
        y = d * lax.rsqrt(ms + EPS) * g_ref[...] * (1.0 - LAMBDA_INIT)
        o_ref[c * cb:(c + 1) * cb, :] = y.T.astype(BF16)


def _slope_parts():
    rest = (2.0 ** (-8.0 * (np.arange(N_HEADS, dtype=np.float64) + 1.0) / N_HEADS) * LOG2E).astype(np.float32)
    pieces = []
    for _ in range(N_SLOPE_PARTS):
        piece = rest.astype(jnp.bfloat16).astype(np.float32)
        pieces.append(piece)
        rest = rest - piece
    return np.stack(pieces, axis=1).reshape(-1)


def _attention(qk, vt, slope_parts, lam, subln_g):
    tq, tk, cb = ATT_TQ, ATT_TK, ATT_CB
    nh = N_HEADS
    nb = 2 * tq // cb
    smem = pl.BlockSpec(memory_space=pltpu.SMEM)
    scratch = [
        pltpu.VMEM((2 * tq, V_HEAD_DIM + AUG_LANES), BF16),
        pltpu.VMEM((tk, AUG_LANES), BF16),
        pltpu.VMEM((1 + tk // cb, tk, cb), F32),
        pltpu.VMEM((ATT_RING, tk, cb), F32),
        pltpu.VMEM((nb, 1, cb), F32),
        pltpu.VMEM((nb, 1, cb), F32),
        pltpu.VMEM((nb, V_HEAD_DIM, cb), F32),
    ]
    vmem = (2 * 2 * SEQ * V_HEAD_DIM * 2 + 4 * tq * V_HEAD_DIM * 2
            + 2 * tq * (V_HEAD_DIM + AUG_LANES) * 2 + tk * AUG_LANES * 2 + (1 + tk // cb) * tk * cb * 4
            + ATT_RING * tk * cb * 4 + 2 * nb * 8 * cb * 4
            + nb * V_HEAD_DIM * cb * 4 + 8 * tk * cb * 4 + (6 << 20))
    return pl.pallas_call(
        _attn_kernel,
        grid=(nh, SEQ // tq),
        in_specs=[
            smem, smem,
            pl.BlockSpec((tq, V_HEAD_DIM), lambda h, i: (i, h)),
            pl.BlockSpec((SEQ, V_HEAD_DIM), lambda h, i: (0, nh + h)),
            pl.BlockSpec((V_HEAD_DIM, SEQ), lambda h, i: (h, 0)),
            pl.BlockSpec((V_HEAD_DIM, 1), lambda h, i: (0, 0)),
        ],
        out_specs=pl.BlockSpec((tq, V_HEAD_DIM), lambda h, i: (i, h)),
        out_shape=jax.ShapeDtypeStruct((SEQ, ATTN_WIDTH), BF16),
        scratch_shapes=scratch,
        compiler_params=_params(("parallel", "arbitrary"), vmem),
        name="diffattn",
    )(slope_parts, lam, qk, qk, vt, subln_g.reshape(V_HEAD_DIM, 1))


def _outproj_kernel(x_ref, gate_ref, gpost_ref, yc_ref, ya_ref, wc_ref, wa_ref, o_ref, vec_ref):
    y = jnp.dot(yc_ref[...], wc_ref[...], preferred_element_type=F32)
    o_ref[...] = y + jnp.dot(ya_ref[...], wa_ref[...], preferred_element_type=F32)
    _gated_residual(x_ref, o_ref, gpost_ref, gate_ref, 1.0, o_ref, vec_ref)


def _outproj(x, mod, g_post, y_conv, y_attn, w_out):
    tm = OUTPROJ_TM
    vmem = (2 * 2 * tm * D_MODEL * 4 + 2 * 2 * tm * CONV_WIDTH * 2 + 2 * D_MODEL * D_MODEL * 2
            + 3 * tm * D_MODEL * 4 + (6 << 20))
    return pl.pallas_call(
        _outproj_kernel,
        grid=(SEQ // tm,),
        in_specs=[
            pl.BlockSpec((tm, D_MODEL), lambda i: (i, 0)),
            pl.BlockSpec((1, D_MODEL), lambda i: (0, 5)),
            pl.BlockSpec((1, D_MODEL), lambda i: (0, 0)),
            pl.BlockSpec((tm, CONV_WIDTH), lambda i: (i, 0)),
            pl.BlockSpec((tm, ATTN_WIDTH), lambda i: (i, 0)),
            pl.BlockSpec((CONV_WIDTH, D_MODEL), lambda i: (0, 0)),
            pl.BlockSpec((ATTN_WIDTH, D_MODEL), lambda i: (1, 0)),
        ],
        out_specs=pl.BlockSpec((tm, D_MODEL), lambda i: (i, 0)),
        out_shape=jax.ShapeDtypeStruct((SEQ, D_MODEL), F32),
        scratch_shapes=[pltpu.VMEM((1, D_MODEL), F32)],
        compiler_params=_params(("parallel",), vmem),
        name="outproj",
    )(x, mod, g_post.reshape(1, D_MODEL), y_conv, y_attn, w_out, w_out)


def kernel(x, c, w_ada, b_ada, g_pre, g_post, w_ffn1_gate, w_ffn1_up, w_ffn1_down, w_in, b_in_conv, w_dw, b_dw, conv_ln_g, conv_ln_b, lam_q1, lam_k1, lam_q2, lam_k2, subln_g, w_out, w_ffn2_gate, w_ffn2_up, w_ffn2_down):
    bf = lambda w: w[0].astype(BF16)
    x2 = x.reshape(SEQ, D_MODEL)

    mod = _ada(c, w_ada[0], b_ada[0])

    x2 = _ffn(x2, mod, 0, g_pre[0, 0], g_post[0, 0], w_ffn1_gate[0], w_ffn1_up[0], w_ffn1_down[0])

    u_conv, qk, vt = _inproj(x2, mod, g_pre[0, 1], bf(w_in))
    y_conv = _conv(u_conv, b_in_conv[0], w_dw[0], b_dw[0], conv_ln_g[0], conv_ln_b[0])

    lam = (jnp.exp(jnp.sum(lam_q1[0].astype(F32) * lam_k1[0].astype(F32)))
           - jnp.exp(jnp.sum(lam_q2[0].astype(F32) * lam_k2[0].astype(F32)))
           + LAMBDA_INIT).reshape(1)
    slope_parts = jnp.asarray(_slope_parts())
    y_attn = _attention(qk, vt, slope_parts, lam, subln_g[0])

    x2 = _outproj(x2, mod, g_post[0, 1], y_conv, y_attn, bf(w_out))

    x2 = _ffn(x2, mod, 6, g_pre[0, 2], g_post[0, 2], w_ffn2_gate[0], w_ffn2_up[0], w_ffn2_down[0])
    return x2.reshape(1, SEQ, D_MODEL)
```

```python
import math

import jax
import jax.numpy as jnp
import numpy as np
from jax import lax
from jax.experimental import pallas as pl
from jax.experimental.pallas import tpu as pltpu

D_MODEL = 2048
SEQ = 8192
CHUNK = 64
CONV_WIDTH = D_MODEL // 2
ATTN_WIDTH = D_MODEL - CONV_WIDTH
CONV_KERNEL = 31
N_HEADS = 8
V_HEAD_DIM = ATTN_WIDTH // N_HEADS
QK_HEAD_DIM = V_HEAD_DIM // 2
D_FF = 5632
N_MOD = 9
IN_WIDTH = 2 * CONV_WIDTH + 3 * ATTN_WIDTH
EPS = 1e-6
NEG_INF = -1e30
LAMBDA_INIT = 0.8 - 0.6 * math.exp(-0.3 * 0)

F32 = jnp.float32
BF16 = jnp.bfloat16

V7X_VMEM_BUDGET_BYTES = 60 * 1024 * 1024

ADA_TN = 1024
FFN_TM = 1024
FFN_TF = 256
INPROJ_TM = 1024
OUTPROJ_TM = 512
PROJ_TN = 1024
CONV_TM = 512
CONV_HALO = 32
CONV_RC = 32
CONV_CC = 512
SUBLANES = 8
ATT_TQ = 2048
ATT_TK = 512
ATT_CB = 256
ATT_SKEW_S = 3
ATT_RING = 4


def _params(semantics, vmem_bytes):
    return pltpu.CompilerParams(
        dimension_semantics=semantics,
        vmem_limit_bytes=min(int(vmem_bytes), V7X_VMEM_BUDGET_BYTES),
    )


ROW_CHUNK = 16


def _norm_rows(src_ref, scale_ref, dst_ref, finish):
    for c in range(src_ref.shape[0] // ROW_CHUNK):
        rows = slice(c * ROW_CHUNK, (c + 1) * ROW_CHUNK)
        x = src_ref[rows, :]
        ms = jnp.mean(x * x, axis=-1, keepdims=True)
        dst_ref[rows, :] = finish(rows, x * lax.rsqrt(ms + EPS) * scale_ref[...]).astype(dst_ref.dtype)


def _modulated_norm(x_ref, gpre_ref, scale_ref, shift_ref, h_ref, vec_ref):
    vec_ref[...] = gpre_ref[...] * (1.0 + scale_ref[...])
    _norm_rows(x_ref, vec_ref, h_ref, lambda rows, y: y + shift_ref[...])


def _gated_residual(x_ref, y_ref, gpost_ref, gate_ref, gate_factor, o_ref, vec_ref):
    vec_ref[...] = gpost_ref[...] * (gate_factor * gate_ref[...])
    _norm_rows(y_ref, vec_ref, o_ref, lambda rows, r: x_ref[rows, :] + r)


def _ada_kernel(c_ref, w_ref, b_ref, o_ref):
    c = c_ref[...]
    s = (c * jax.nn.sigmoid(c)).astype(BF16)
    s8 = jnp.broadcast_to(s, (8, D_MODEL))
    r = jnp.dot(s8, w_ref[...].astype(BF16), preferred_element_type=F32)
    o_ref[...] = r[0:1, :] + b_ref[...]


def _ada(c, w, b):
    n = w.shape[1]
    return pl.pallas_call(
        _ada_kernel,
        grid=(n // ADA_TN,),
        in_specs=[
            pl.BlockSpec((1, D_MODEL), lambda j: (0, 0)),
            pl.BlockSpec((D_MODEL, ADA_TN), lambda j: (0, j)),
            pl.BlockSpec((1, ADA_TN), lambda j: (0, j)),
        ],
        out_specs=pl.BlockSpec((1, ADA_TN), lambda j: (0, j)),
        out_shape=jax.ShapeDtypeStruct((1, n), F32),
        compiler_params=_params(("parallel",), 2 * D_MODEL * ADA_TN * 4 + D_MODEL * ADA_TN * 2 + (8 << 20)),
        name="adaln",
    )(c, w, b.reshape(1, n))


def _ffn_kernel(x_ref, shift_ref, scale_ref, gate_ref, gpre_ref, gpost_ref,
                wg_ref, wu_ref, wd_ref, o_ref, h_ref, vec_ref):
    j = pl.program_id(1)

    @pl.when(j == 0)
    def _():
        _modulated_norm(x_ref, gpre_ref, scale_ref, shift_ref, h_ref, vec_ref)
        o_ref[...] = jnp.zeros_like(o_ref)

    h = h_ref[...]
    g = jnp.dot(h, wg_ref[...].astype(BF16), preferred_element_type=F32)
    u = jnp.dot(h, wu_ref[...].astype(BF16), preferred_element_type=F32)
    a = (g * jax.nn.sigmoid(g) * u).astype(BF16)
    o_ref[...] += jnp.dot(a, wd_ref[...].astype(BF16), preferred_element_type=F32)

    @pl.when(j == pl.num_programs(1) - 1)
    def _():
        _gated_residual(x_ref, o_ref, gpost_ref, gate_ref, 0.5, o_ref, vec_ref)


def _ffn(x, mod, mod_base, g_pre, g_post, wg, wu, wd):
    tm, tf = FFN_TM, FFN_TF
    row = lambda k: pl.BlockSpec((1, D_MODEL), lambda i, j, k=k: (0, k))
    vec = pl.BlockSpec((1, D_MODEL), lambda i, j: (0, 0))
    vmem = (2 * 2 * tm * D_MODEL * 4
            + tm * D_MODEL * 2
            + 2 * 3 * D_MODEL * tf * 4
            + 3 * D_MODEL * tf * 2
            + 3 * tm * tf * 4
            + (4 << 20))
    return pl.pallas_call(
        _ffn_kernel,
        grid=(SEQ // tm, D_FF // tf),
        in_specs=[
            pl.BlockSpec((tm, D_MODEL), lambda i, j: (i, 0)),
            row(mod_base), row(mod_base + 1), row(mod_base + 2),
            vec, vec,
            pl.BlockSpec((D_MODEL, tf), lambda i, j: (0, j)),
            pl.BlockSpec((D_MODEL, tf), lambda i, j: (0, j)),
            pl.BlockSpec((tf, D_MODEL), lambda i, j: (j, 0)),
        ],
        out_specs=pl.BlockSpec((tm, D_MODEL), lambda i, j: (i, 0)),
        out_shape=jax.ShapeDtypeStruct((SEQ, D_MODEL), F32),
        scratch_shapes=[pltpu.VMEM((tm, D_MODEL), BF16), pltpu.VMEM((1, D_MODEL), F32)],
        compiler_params=_params(("parallel", "arbitrary"), vmem),
        name="ffn",
    )(x, mod, mod, mod, g_pre.reshape(1, D_MODEL), g_post.reshape(1, D_MODEL), wg, wu, wd)


N_U_BLOCKS = 2 * CONV_WIDTH // PROJ_TN
assert N_U_BLOCKS == 2
N_QK_BLOCKS = 2 * ATTN_WIDTH // PROJ_TN
assert PROJ_TN == ATTN_WIDTH
LOG2E = math.log2(math.e)
Q_SCALE_LOG2 = QK_HEAD_DIM ** -0.5 * LOG2E


def _inproj_kernel(x_ref, shift_ref, scale_ref, gpre_ref, w_ref, bin_ref, glu_ref, qk_ref, vt_ref,
                   h_ref, vec_ref, ua_ref):
    j = pl.program_id(1)

    @pl.when(j == 0)
    def _():
        _modulated_norm(x_ref, gpre_ref, scale_ref, shift_ref, h_ref, vec_ref)
        ua_ref[...] = jnp.dot(h_ref[...], w_ref[...], preferred_element_type=F32)

    @pl.when(j == 1)
    def _():
        ug = jnp.dot(h_ref[...], w_ref[...], preferred_element_type=F32)
        a = ua_ref[...] + bin_ref[:, 0:CONV_WIDTH]
        glu_ref[...] = a * jax.nn.sigmoid(ug + bin_ref[:, CONV_WIDTH:])

    @pl.when(jnp.logical_and(j >= N_U_BLOCKS, j < N_U_BLOCKS + N_QK_BLOCKS))
    def _():
        c = jnp.where(j == N_U_BLOCKS, Q_SCALE_LOG2, 1.0)
        qk_ref[...] = (jnp.dot(h_ref[...], w_ref[...], preferred_element_type=F32) * c).astype(BF16)

    @pl.when(j == N_U_BLOCKS + N_QK_BLOCKS)
    def _():
        v = jnp.dot(h_ref[...], w_ref[...], preferred_element_type=F32)
        vt_ref[...] = v.T.astype(BF16)


def _inproj(x, mod, g_pre, w_in, b_in):
    tm, tn = INPROJ_TM, PROJ_TN
    n_main = N_U_BLOCKS + N_QK_BLOCKS
    vmem = (2 * tm * D_MODEL * 4 + tm * D_MODEL * 2 + 2 * D_MODEL * tn * 2
            + 2 * tm * tn * (4 + 2) + 2 * ATTN_WIDTH * tm * 2 + 3 * tm * tn * 4 + (4 << 20))
    return pl.pallas_call(
        _inproj_kernel,
        grid=(SEQ // tm, n_main + 1),
        in_specs=[
            pl.BlockSpec((tm, D_MODEL), lambda i, j: (i, 0)),
            pl.BlockSpec((1, D_MODEL), lambda i, j: (0, 3)),
            pl.BlockSpec((1, D_MODEL), lambda i, j: (0, 4)),
            pl.BlockSpec((1, D_MODEL), lambda i, j: (0, 0)),
            pl.BlockSpec((D_MODEL, tn), lambda i, j: (0, j)),
            pl.BlockSpec((1, 2 * CONV_WIDTH), lambda i, j: (0, 0)),
        ],
        out_specs=[
            pl.BlockSpec((tm, CONV_WIDTH), lambda i, j: (i, 0)),
            pl.BlockSpec((tm, tn), lambda i, j: (i, jnp.clip(j - N_U_BLOCKS, 0, N_QK_BLOCKS - 1))),
            pl.BlockSpec((ATTN_WIDTH, tm), lambda i, j: (0, i)),
        ],
        out_shape=[
            jax.ShapeDtypeStruct((SEQ, CONV_WIDTH), F32),
            jax.ShapeDtypeStruct((SEQ, 2 * ATTN_WIDTH), BF16),
            jax.ShapeDtypeStruct((ATTN_WIDTH, SEQ), BF16),
        ],
        scratch_shapes=[pltpu.VMEM((tm, D_MODEL), BF16), pltpu.VMEM((1, D_MODEL), F32),
                        pltpu.VMEM((tm, CONV_WIDTH), F32)],
        compiler_params=_params(("parallel", "arbitrary"), vmem),
        name="inproj",
    )(x, mod, mod, g_pre.reshape(1, D_MODEL), w_in, b_in.reshape(1, -1))


def _conv_kernel(v_ref, vprev_ref, wdw_ref, bdw_ref, lng_ref, lnb_ref, o_ref,
                 vbuf_ref, conv_ref, wb_ref):
    i = pl.program_id(0)
    vbuf_ref[0, 0:CONV_HALO, :] = jnp.where(i > 0, vprev_ref[...], 0.0)
    vbuf_ref[0, CONV_HALO:, :] = v_ref[...]
    n_shift = CONV_TM + CONV_HALO - SUBLANES
    for s in range(1, SUBLANES):
        vbuf_ref[s, 0:n_shift, :] = vbuf_ref[0, s:s + n_shift, :]

    for j in range(CONV_KERNEL):
        wb_ref[j] = jnp.broadcast_to(wdw_ref[j:j + 1, :], (SUBLANES, CONV_WIDTH))

    lead = CONV_HALO - (CONV_KERNEL - 1)
    for cb in range(CONV_WIDTH // CONV_CC):
        cs = slice(cb * CONV_CC, (cb + 1) * CONV_CC)
        for r in range(CONV_TM // CONV_RC):
            accs = [jnp.zeros((SUBLANES, CONV_CC), F32) for _ in range(CONV_RC // SUBLANES)]
            for j in range(CONV_KERNEL):
                s, lo = (lead + j) % SUBLANES, r * CONV_RC + (lead + j) // SUBLANES * SUBLANES
                w = wb_ref[j, :, cs]
                for q in range(len(accs)):
                    accs[q] = accs[q] + vbuf_ref[s, lo + q * SUBLANES:lo + (q + 1) * SUBLANES, cs] * w
            for q, acc in enumerate(accs):
                conv_ref[r * CONV_RC + q * SUBLANES:r * CONV_RC + (q + 1) * SUBLANES, cs] = acc

    v = conv_ref[...] + bdw_ref[...]
    mu = jnp.mean(v, axis=-1, keepdims=True)
    xc = v - mu
    var = jnp.mean(xc * xc, axis=-1, keepdims=True)
    y = xc * lax.rsqrt(var + EPS) * lng_ref[...] + lnb_ref[...]
    o_ref[...] = (y * jax.nn.sigmoid(y)).astype(BF16)


def _conv(v, w_dw, b_dw, ln_g, ln_b):
    tm = CONV_TM
    halo_blocks = tm // CONV_HALO
    vec = lambda n: pl.BlockSpec((1, n), lambda i: (0, 0))
    vmem = (2 * tm * CONV_WIDTH * 4 + 2 * CONV_HALO * CONV_WIDTH * 4
            + (SUBLANES * (tm + CONV_HALO) + tm) * CONV_WIDTH * 4 + 2 * tm * CONV_WIDTH * 2
            + 6 * tm * CONV_WIDTH * 4 + (6 << 20))
    return pl.pallas_call(
        _conv_kernel,
        grid=(SEQ // tm,),
        in_specs=[
            pl.BlockSpec((tm, CONV_WIDTH), lambda i: (i, 0)),
            pl.BlockSpec((CONV_HALO, CONV_WIDTH), lambda i: (jnp.maximum(i * halo_blocks - 1, 0), 0)),
            pl.BlockSpec((CONV_KERNEL, CONV_WIDTH), lambda i: (0, 0)),
            vec(CONV_WIDTH), vec(CONV_WIDTH), vec(CONV_WIDTH),
        ],
        out_specs=pl.BlockSpec((tm, CONV_WIDTH), lambda i: (i, 0)),
        out_shape=jax.ShapeDtypeStruct((SEQ, CONV_WIDTH), BF16),
        scratch_shapes=[pltpu.VMEM((SUBLANES, tm + CONV_HALO, CONV_WIDTH), F32),
                        pltpu.VMEM((tm, CONV_WIDTH), F32),
                        pltpu.VMEM((CONV_KERNEL, SUBLANES, CONV_WIDTH), F32)],
        compiler_params=_params(("parallel",), vmem),
        name="convbranch",
    )(v, v, w_dw, b_dw.reshape(1, -1), ln_g.reshape(1, -1), ln_b.reshape(1, -1))


N_SLOPE_PARTS = 3
AUG_LANES = 128
POS_SPLIT = 256
assert POS_SPLIT <= 256 and ATT_TQ // POS_SPLIT <= 256
assert ATT_TQ % ATT_TK == 0 and ATT_TK % ATT_CB == 0 and ATT_CB % CHUNK == 0


def _aug_operand(n_rows, parts, key_side):
    lane = lax.broadcasted_iota(jnp.int32, (n_rows, AUG_LANES), 1)
    r = lax.broadcasted_iota(jnp.int32, (n_rows, AUG_LANES), 0)
    hi = ((r // POS_SPLIT) * POS_SPLIT).astype(F32)
    lo = (r % POS_SPLIT).astype(F32)
    piece = lane % N_SLOPE_PARTS
    part = jnp.where(piece == 0, parts[0], jnp.where(piece == 1, parts[1], parts[2]))
    zero = jnp.zeros((n_rows, AUG_LANES), F32)
    if key_side:
        vals = (-part, -part, hi, lo)
    else:
        vals = (hi, lo, part, part)
    out = zero
    for g, v in enumerate(vals):
        out = jnp.where(lane // N_SLOPE_PARTS == g, v, out)
    return out


def _diagonal_blocks():
    tq, tk, cb = ATT_TQ, ATT_TK, ATT_CB
    nqc = tq // cb
    out = []
    for d in range(tq // tk):
        for b in range(2 * nqc):
            q_lo = (b % nqc) * cb
            if q_lo + cb <= d * tk:
                continue
            out.append((d, b, None if q_lo >= (d + 1) * tk else 1 + (q_lo - d * tk) // cb))
    return out


def _attn_kernel(parts_ref, lam_ref, q_ref, k_ref, vt_ref, g_ref, o_ref,
                 qs_ref, kaug_ref, corr_ref, s_ref, m_ref, l_ref, acc_ref):
    tq, tk, cb = ATT_TQ, ATT_TK, ATT_CB
    nqc = tq // cb
    nb = 2 * nqc
    h = pl.program_id(0)
    i = pl.program_id(1)
    n_before = i * (tq // tk)
    parts = [parts_ref[h * N_SLOPE_PARTS + n] for n in range(N_SLOPE_PARTS)]
    sl = parts[0] + parts[1] + parts[2]

    @pl.when(i == 0)
    def _():
        kaug_ref[...] = _aug_operand(tk, parts, key_side=True).astype(BF16)
        qaug = _aug_operand(tq, parts, key_side=False).astype(BF16)
        qs_ref[0:tq, V_HEAD_DIM:] = qaug
        qs_ref[tq:, V_HEAD_DIM:] = qaug
        corr_ref[0] = jnp.zeros((tk, cb), F32)
        for c in range(tk // cb):
            krel = lax.broadcasted_iota(jnp.int32, (tk, cb), 0)
            qrel = lax.broadcasted_iota(jnp.int32, (tk, cb), 1) + c * cb
            after = jnp.maximum(krel - qrel, 0).astype(F32)
            allowed = (krel // CHUNK) <= (qrel // CHUNK)
            corr_ref[1 + c] = jnp.where(allowed, -2.0 * sl * after, NEG_INF)

    q = q_ref[...]
    lane = lax.broadcasted_iota(jnp.int32, (tq, V_HEAD_DIM), 1)
    zero = jnp.zeros_like(q)
    qs_ref[0:tq, 0:V_HEAD_DIM] = jnp.where(lane < QK_HEAD_DIM, q, zero)
    qs_ref[tq:, 0:V_HEAD_DIM] = jnp.where(lane >= QK_HEAD_DIM, q, zero)

    m_ref[...] = jnp.full_like(m_ref, NEG_INF)
    l_ref[...] = jnp.zeros_like(l_ref)
    acc_ref[...] = jnp.zeros_like(acc_ref)

    def scores(t, b, slot):
        k = k_ref[pl.ds(pl.multiple_of(t * tk, tk), tk), :]
        kx = jnp.concatenate([k, kaug_ref[...]], axis=1)
        s_ref[slot] = lax.dot_general(kx, qs_ref[b * cb:(b + 1) * cb, :], (((1,), (1,)), ((), ())),
                                      preferred_element_type=F32)

    def value_tile(t):
        return vt_ref[:, pl.ds(pl.multiple_of(t * tk, tk), tk)]

    def softmax_values(t, b, slot, corr, vt):
        off = sl * (i * tq - t * tk).astype(F32)
        s = s_ref[slot]
        if corr is not None:
            s = s + corr_ref[corr]
        m_prev = m_ref[b]
        m_new = jnp.maximum(m_prev, jnp.max(s, axis=0, keepdims=True) - off)
        alpha = jnp.exp2(m_prev - m_new)
        p = jnp.exp2(s - (m_new + off))
        l_ref[b] = alpha * l_ref[b] + jnp.sum(p, axis=0, keepdims=True)
        m_ref[b] = m_new
        pv = jnp.dot(vt, p.astype(BF16), preferred_element_type=F32)
        acc_ref[b] = alpha * acc_ref[b] + pv

    blocks = _diagonal_blocks()
    assert [blk[:2] for blk in blocks[:ATT_SKEW_S]] == [(0, b) for b in range(ATT_SKEW_S)]
    assert nb % ATT_RING == 0

    for b in range(ATT_SKEW_S):
        scores(0, b, b % ATT_RING)

    tiles_per_iter = tq // tk

    def body(jj, carry):
        for u in range(tiles_per_iter):
            j = jj * tiles_per_iter + u
            vt = value_tile(j)
            for b in range(nb):
                bs = b + ATT_SKEW_S
                scores(j + bs // nb, bs % nb, bs % ATT_RING)
                softmax_values(j, b, b % ATT_RING, None, vt)
        return carry

    lax.fori_loop(0, i, body, 0)

    vts = {}
    for n, (d, b, corr) in enumerate(blocks):
        if d not in vts:
            vts[d] = value_tile(n_before + d)
        if n + ATT_SKEW_S < len(blocks):
            d2, b2, _ = blocks[n + ATT_SKEW_S]
            scores(n_before + d2, b2, (n + ATT_SKEW_S) % ATT_RING)
        softmax_values(n_before + d, b, n % ATT_RING, corr, vts[d])

    lam = lam_ref[0]
    for c in range(nqc):
        d = acc_ref[c] / l_ref[c] - lam * (acc_ref[nqc + c] / l_ref[nqc + c])
        ms = jnp.mean(d * d, axis=0, keepdims=True)
        y = d * lax.rsqrt(ms + EPS) * g_ref[...] * (1.0 - LAMBDA_INIT)
        o_ref[c * cb:(c + 1) * cb, :] = y.T.astype(BF16)


def _slope_parts():
    rest = (2.0 ** (-8.0 * (np.arange(N_HEADS, dtype=np.float64) + 1.0) / N_HEADS) * LOG2E).astype(np.float32)
    pieces = []
    for _ in range(N_SLOPE_PARTS):
        piece = rest.astype(jnp.bfloat16).astype(np.float32)
        pieces.append(piece)
        rest = rest - piece
    return np.stack(pieces, axis=1).reshape(-1)


def _attention(qk, vt, slope_parts, lam, subln_g):
    tq, tk, cb = ATT_TQ, ATT_TK, ATT_CB
    nh = N_HEADS
    nb = 2 * tq // cb
    smem = pl.BlockSpec(memory_space=pltpu.SMEM)
    scratch = [
        pltpu.VMEM((2 * tq, V_HEAD_DIM + AUG_LANES), BF16),
        pltpu.VMEM((tk, AUG_LANES), BF16),
        pltpu.VMEM((1 + tk // cb, tk, cb), F32),
        pltpu.VMEM((ATT_RING, tk, cb), F32),
        pltpu.VMEM((nb, 1, cb), F32),
        pltpu.VMEM((nb, 1, cb), F32),
        pltpu.VMEM((nb, V_HEAD_DIM, cb), F32),
    ]
    vmem = (2 * 2 * SEQ * V_HEAD_DIM * 2 + 4 * tq * V_HEAD_DIM * 2
            + 2 * tq * (V_HEAD_DIM + AUG_LANES) * 2 + tk * AUG_LANES * 2 + (1 + tk // cb) * tk * cb * 4
            + ATT_RING * tk * cb * 4 + 2 * nb * 8 * cb * 4
            + nb * V_HEAD_DIM * cb * 4 + 8 * tk * cb * 4 + (6 << 20))
    return pl.pallas_call(
        _attn_kernel,
        grid=(nh, SEQ // tq),
        in_specs=[
            smem, smem,
            pl.BlockSpec((tq, V_HEAD_DIM), lambda h, i: (i, h)),
            pl.BlockSpec((SEQ, V_HEAD_DIM), lambda h, i: (0, nh + h)),
            pl.BlockSpec((V_HEAD_DIM, SEQ), lambda h, i: (h, 0)),
            pl.BlockSpec((V_HEAD_DIM, 1), lambda h, i: (0, 0)),
        ],
        out_specs=pl.BlockSpec((tq, V_HEAD_DIM), lambda h, i: (i, h)),
        out_shape=jax.ShapeDtypeStruct((SEQ, ATTN_WIDTH), BF16),
        scratch_shapes=scratch,
        compiler_params=_params(("parallel", "arbitrary"), vmem),
        name="diffattn",
    )(slope_parts, lam, qk, qk, vt, subln_g.reshape(V_HEAD_DIM, 1))


def _outproj_kernel(x_ref, gate_ref, gpost_ref, yc_ref, ya_ref, wc_ref, wa_ref, o_ref, vec_ref):
    y = jnp.dot(yc_ref[...], wc_ref[...], preferred_element_type=F32)
    o_ref[...] = y + jnp.dot(ya_ref[...], wa_ref[...], preferred_element_type=F32)
    _gated_residual(x_ref, o_ref, gpost_ref, gate_ref, 1.0, o_ref, vec_ref)


def _outproj(x, mod, g_post, y_conv, y_attn, w_out):
    tm = OUTPROJ_TM
    vmem = (2 * 2 * tm * D_MODEL * 4 + 2 * 2 * tm * CONV_WIDTH * 2 + 2 * D_MODEL * D_MODEL * 2
            + 3 * tm * D_MODEL * 4 + (6 << 20))
    return pl.pallas_call(
        _outproj_kernel,
        grid=(SEQ // tm,),
        in_specs=[
            pl.BlockSpec((tm, D_MODEL), lambda i: (i, 0)),
            pl.BlockSpec((1, D_MODEL), lambda i: (0, 5)),
            pl.BlockSpec((1, D_MODEL), lambda i: (0, 0)),
            pl.BlockSpec((tm, CONV_WIDTH), lambda i: (i, 0)),
            pl.BlockSpec((tm, ATTN_WIDTH), lambda i: (i, 0)),
            pl.BlockSpec((CONV_WIDTH, D_MODEL), lambda i: (0, 0)),
            pl.BlockSpec((ATTN_WIDTH, D_MODEL), lambda i: (1, 0)),
        ],
        out_specs=pl.BlockSpec((tm, D_MODEL), lambda i: (i, 0)),
        out_shape=jax.ShapeDtypeStruct((SEQ, D_MODEL), F32),
        scratch_shapes=[pltpu.VMEM((1, D_MODEL), F32)],
        compiler_params=_params(("parallel",), vmem),
        name="outproj",
    )(x, mod, g_post.reshape(1, D_MODEL), y_conv, y_attn, w_out, w_out)


def kernel(x, c, w_ada, b_ada, g_pre, g_post, w_ffn1_gate, w_ffn1_up, w_ffn1_down, w_in, b_in_conv, w_dw, b_dw, conv_ln_g, conv_ln_b, lam_q1, lam_k1, lam_q2, lam_k2, subln_g, w_out, w_ffn2_gate, w_ffn2_up, w_ffn2_down):
    bf = lambda w: w[0].astype(BF16)
    x2 = x.reshape(SEQ, D_MODEL)

    mod = _ada(c, w_ada[0], b_ada[0])

    x2 = _ffn(x2, mod, 0, g_pre[0, 0], g_post[0, 0], w_ffn1_gate[0], w_ffn1_up[0], w_ffn1_down[0])

    v_conv, qk, vt = _inproj(x2, mod, g_pre[0, 1], bf(w_in), b_in_conv[0])
    y_conv = _conv(v_conv, w_dw[0], b_dw[0], conv_ln_g[0], conv_ln_b[0])

    lam = (jnp.exp(jnp.sum(lam_q1[0].astype(F32) * lam_k1[0].astype(F32)))
           - jnp.exp(jnp.sum(lam_q2[0].astype(F32) * lam_k2[0].astype(F32)))
           + LAMBDA_INIT).reshape(1)
    slope_parts = jnp.asarray(_slope_parts())
    y_attn = _attention(qk, vt, slope_parts, lam, subln_g[0])

    x2 = _outproj(x2, mod, g_post[0, 1], y_conv, y_attn, bf(w_out))

    x2 = _ffn(x2, mod, 6, g_pre[0, 2], g_post[0, 2], w_ffn2_gate[0], w_ffn2_up[0], w_ffn2_down[0])
    return x2.reshape(1, SEQ, D_MODEL)
```

```python
import math

import jax
import jax.numpy as jnp
import numpy as np
from jax import lax
from jax.experimental import pallas as pl
from jax.experimental.pallas import tpu as pltpu

D_MODEL = 2048
SEQ = 8192
CHUNK = 64
CONV_WIDTH = D_MODEL // 2
ATTN_WIDTH = D_MODEL - CONV_WIDTH
CONV_KERNEL = 31
N_HEADS = 8
V_HEAD_DIM = ATTN_WIDTH // N_HEADS
QK_HEAD_DIM = V_HEAD_DIM // 2
D_FF = 5632
N_MOD = 9
IN_WIDTH = 2 * CONV_WIDTH + 3 * ATTN_WIDTH
EPS = 1e-6
NEG_INF = -1e30
LAMBDA_INIT = 0.8 - 0.6 * math.exp(-0.3 * 0)

F32 = jnp.float32
BF16 = jnp.bfloat16

V7X_VMEM_BUDGET_BYTES = 62 * 1024 * 1024

ADA_TN = 2048
FFN_TM = 1024
FFN_TF = 256
INPROJ_TM = 1024
OUTPROJ_TM = 512
PROJ_TN = 1024
CONV_TM = 512
CONV_HALO = 32
CONV_RC = 32
CONV_CC = 512
SUBLANES = 8
ATT_TQ = 2048
ATT_TK = 512
ATT_CB = 256
ATT_SKEW_S = 3
ATT_RING = 4


def _params(semantics, vmem_bytes):
    return pltpu.CompilerParams(
        dimension_semantics=semantics,
        vmem_limit_bytes=min(int(vmem_bytes), V7X_VMEM_BUDGET_BYTES),
    )


ROW_CHUNK = 16


def _norm_rows(src_ref, scale_ref, dst_ref, finish):
    for c in range(src_ref.shape[0] // ROW_CHUNK):
        rows = slice(c * ROW_CHUNK, (c + 1) * ROW_CHUNK)
        x = src_ref[rows, :]
        ms = jnp.mean(x * x, axis=-1, keepdims=True)
        dst_ref[rows, :] = finish(rows, x * lax.rsqrt(ms + EPS) * scale_ref[...]).astype(dst_ref.dtype)


def _modulated_norm(x_ref, gpre_ref, scale_ref, shift_ref, h_ref, vec_ref):
    vec_ref[...] = gpre_ref[...] * (1.0 + scale_ref[...])
    _norm_rows(x_ref, vec_ref, h_ref, lambda rows, y: y + shift_ref[...])


def _gated_residual(x_ref, y_ref, gpost_ref, gate_ref, gate_factor, o_ref, vec_ref):
    vec_ref[...] = gpost_ref[...] * (gate_factor * gate_ref[...])
    _norm_rows(y_ref, vec_ref, o_ref, lambda rows, r: x_ref[rows, :] + r)


def _ada_kernel(c_ref, w_ref, b_ref, o_ref):
    c = c_ref[...]
    s = (c * jax.nn.sigmoid(c)).astype(BF16)
    s8 = jnp.broadcast_to(s, (8, D_MODEL))
    r = jnp.dot(s8, w_ref[...].astype(BF16), preferred_element_type=F32)
    o_ref[...] = r[0:1, :] + b_ref[...]


def _ada(c, w, b):
    n = w.shape[1]
    return pl.pallas_call(
        _ada_kernel,
        grid=(n // ADA_TN,),
        in_specs=[
            pl.BlockSpec((1, D_MODEL), lambda j: (0, 0)),
            pl.BlockSpec((D_MODEL, ADA_TN), lambda j: (0, j)),
            pl.BlockSpec((1, ADA_TN), lambda j: (0, j)),
        ],
        out_specs=pl.BlockSpec((1, ADA_TN), lambda j: (0, j)),
        out_shape=jax.ShapeDtypeStruct((1, n), F32),
        compiler_params=_params(("parallel",), 2 * D_MODEL * ADA_TN * 4 + D_MODEL * ADA_TN * 2 + (8 << 20)),
        name="adaln",
    )(c, w, b.reshape(1, n))


def _ffn_kernel(x_ref, shift_ref, scale_ref, gate_ref, gpre_ref, gpost_ref,
                wg_ref, wu_ref, wd_ref, o_ref, h_ref, vec_ref):
    j = pl.program_id(1)

    @pl.when(j == 0)
    def _():
        _modulated_norm(x_ref, gpre_ref, scale_ref, shift_ref, h_ref, vec_ref)
        o_ref[...] = jnp.zeros_like(o_ref)

    h = h_ref[...]
    g = jnp.dot(h, wg_ref[...].astype(BF16), preferred_element_type=F32)
    u = jnp.dot(h, wu_ref[...].astype(BF16), preferred_element_type=F32)
    a = (g * jax.nn.sigmoid(g) * u).astype(BF16)
    o_ref[...] += jnp.dot(a, wd_ref[...].astype(BF16), preferred_element_type=F32)

    @pl.when(j == pl.num_programs(1) - 1)
    def _():
        _gated_residual(x_ref, o_ref, gpost_ref, gate_ref, 0.5, o_ref, vec_ref)


def _ffn(x, mod, mod_base, g_pre, g_post, wg, wu, wd):
    tm, tf = FFN_TM, FFN_TF
    row = lambda k: pl.BlockSpec((1, D_MODEL), lambda i, j, k=k: (0, k))
    vec = pl.BlockSpec((1, D_MODEL), lambda i, j: (0, 0))
    vmem = (2 * 2 * tm * D_MODEL * 4
            + tm * D_MODEL * 2
            + 2 * 3 * D_MODEL * tf * 4
            + 3 * D_MODEL * tf * 2
            + 3 * tm * tf * 4
            + (4 << 20))
    return pl.pallas_call(
        _ffn_kernel,
        grid=(SEQ // tm, D_FF // tf),
        in_specs=[
            pl.BlockSpec((tm, D_MODEL), lambda i, j: (i, 0)),
            row(mod_base), row(mod_base + 1), row(mod_base + 2),
            vec, vec,
            pl.BlockSpec((D_MODEL, tf), lambda i, j: (0, j)),
            pl.BlockSpec((D_MODEL, tf), lambda i, j: (0, j)),
            pl.BlockSpec((tf, D_MODEL), lambda i, j: (j, 0)),
        ],
        out_specs=pl.BlockSpec((tm, D_MODEL), lambda i, j: (i, 0)),
        out_shape=jax.ShapeDtypeStruct((SEQ, D_MODEL), F32),
        scratch_shapes=[pltpu.VMEM((tm, D_MODEL), BF16), pltpu.VMEM((1, D_MODEL), F32)],
        compiler_params=_params(("parallel", "arbitrary"), vmem),
        name="ffn",
    )(x, mod, mod, mod, g_pre.reshape(1, D_MODEL), g_post.reshape(1, D_MODEL), wg, wu, wd)


N_U_BLOCKS = 2 * CONV_WIDTH // PROJ_TN
assert N_U_BLOCKS == 2
N_QK_BLOCKS = 2 * ATTN_WIDTH // PROJ_TN
assert PROJ_TN == ATTN_WIDTH
LOG2E = math.log2(math.e)
Q_SCALE_LOG2 = QK_HEAD_DIM ** -0.5 * LOG2E


def _inproj_kernel(x_ref, shift_ref, scale_ref, gpre_ref, w_ref, bin_ref, glu_ref, qk_ref, vt_ref,
                   h_ref, vec_ref, ua_ref):
    j = pl.program_id(1)

    @pl.when(j == 0)
    def _():
        _modulated_norm(x_ref, gpre_ref, scale_ref, shift_ref, h_ref, vec_ref)
        ua_ref[...] = jnp.dot(h_ref[...], w_ref[...].astype(BF16), preferred_element_type=F32)

    @pl.when(j == 1)
    def _():
        ug = jnp.dot(h_ref[...], w_ref[...].astype(BF16), preferred_element_type=F32)
        a = ua_ref[...] + bin_ref[:, 0:CONV_WIDTH]
        glu_ref[...] = a * jax.nn.sigmoid(ug + bin_ref[:, CONV_WIDTH:])

    @pl.when(jnp.logical_and(j >= N_U_BLOCKS, j < N_U_BLOCKS + N_QK_BLOCKS))
    def _():
        c = jnp.where(j == N_U_BLOCKS, Q_SCALE_LOG2, 1.0)
        qk_ref[...] = (jnp.dot(h_ref[...], w_ref[...].astype(BF16), preferred_element_type=F32) * c).astype(BF16)

    @pl.when(j == N_U_BLOCKS + N_QK_BLOCKS)
    def _():
        v = jnp.dot(h_ref[...], w_ref[...].astype(BF16), preferred_element_type=F32)
        vt_ref[...] = v.T.astype(BF16)


def _inproj(x, mod, g_pre, w_in, b_in):
    tm, tn = INPROJ_TM, PROJ_TN
    n_main = N_U_BLOCKS + N_QK_BLOCKS
    vmem = (2 * tm * D_MODEL * 4 + tm * D_MODEL * 2 + 2 * D_MODEL * tn * 4 + D_MODEL * tn * 2
            + 2 * tm * tn * (4 + 2) + 2 * ATTN_WIDTH * tm * 2 + 3 * tm * tn * 4 + (4 << 20))
    return pl.pallas_call(
        _inproj_kernel,
        grid=(SEQ // tm, n_main + 1),
        in_specs=[
            pl.BlockSpec((tm, D_MODEL), lambda i, j: (i, 0)),
            pl.BlockSpec((1, D_MODEL), lambda i, j: (0, 3)),
            pl.BlockSpec((1, D_MODEL), lambda i, j: (0, 4)),
            pl.BlockSpec((1, D_MODEL), lambda i, j: (0, 0)),
            pl.BlockSpec((D_MODEL, tn), lambda i, j: (0, j)),
            pl.BlockSpec((1, 2 * CONV_WIDTH), lambda i, j: (0, 0)),
        ],
        out_specs=[
            pl.BlockSpec((tm, CONV_WIDTH), lambda i, j: (i, 0)),
            pl.BlockSpec((tm, tn), lambda i, j: (i, jnp.clip(j - N_U_BLOCKS, 0, N_QK_BLOCKS - 1))),
            pl.BlockSpec((ATTN_WIDTH, tm), lambda i, j: (0, i)),
        ],
        out_shape=[
            jax.ShapeDtypeStruct((SEQ, CONV_WIDTH), F32),
            jax.ShapeDtypeStruct((SEQ, 2 * ATTN_WIDTH), BF16),
            jax.ShapeDtypeStruct((ATTN_WIDTH, SEQ), BF16),
        ],
        scratch_shapes=[pltpu.VMEM((tm, D_MODEL), BF16), pltpu.VMEM((1, D_MODEL), F32),
                        pltpu.VMEM((tm, CONV_WIDTH), F32)],
        compiler_params=_params(("parallel", "arbitrary"), vmem),
        name="inproj",
    )(x, mod, mod, g_pre.reshape(1, D_MODEL), w_in, b_in.reshape(1, -1))


def _conv_kernel(v_ref, vprev_ref, wdw_ref, bdw_ref, lng_ref, lnb_ref, o_ref,
                 vbuf_ref, conv_ref, wb_ref):
    i = pl.program_id(0)
    vbuf_ref[0, 0:CONV_HALO, :] = jnp.where(i > 0, vprev_ref[...], 0.0)
    vbuf_ref[0, CONV_HALO:, :] = v_ref[...]
    n_shift = CONV_TM + CONV_HALO - SUBLANES
    for s in range(1, SUBLANES):
        vbuf_ref[s, 0:n_shift, :] = vbuf_ref[0, s:s + n_shift, :]

    for j in range(CONV_KERNEL):
        wb_ref[j] = jnp.broadcast_to(wdw_ref[j:j + 1, :], (SUBLANES, CONV_WIDTH))

    lead = CONV_HALO - (CONV_KERNEL - 1)
    for cb in range(CONV_WIDTH // CONV_CC):
        cs = slice(cb * CONV_CC, (cb + 1) * CONV_CC)
        for r in range(CONV_TM // CONV_RC):
            accs = [jnp.zeros((SUBLANES, CONV_CC), F32) for _ in range(CONV_RC // SUBLANES)]
            for j in range(CONV_KERNEL):
                s, lo = (lead + j) % SUBLANES, r * CONV_RC + (lead + j) // SUBLANES * SUBLANES
                w = wb_ref[j, :, cs]
                for q in range(len(accs)):
                    accs[q] = accs[q] + vbuf_ref[s, lo + q * SUBLANES:lo + (q + 1) * SUBLANES, cs] * w
            for q, acc in enumerate(accs):
                conv_ref[r * CONV_RC + q * SUBLANES:r * CONV_RC + (q + 1) * SUBLANES, cs] = acc

    v = conv_ref[...] + bdw_ref[...]
    mu = jnp.mean(v, axis=-1, keepdims=True)
    xc = v - mu
    var = jnp.mean(xc * xc, axis=-1, keepdims=True)
    y = xc * lax.rsqrt(var + EPS) * lng_ref[...] + lnb_ref[...]
    o_ref[...] = (y * jax.nn.sigmoid(y)).astype(BF16)


def _conv(v, w_dw, b_dw, ln_g, ln_b):
    tm = CONV_TM
    halo_blocks = tm // CONV_HALO
    vec = lambda n: pl.BlockSpec((1, n), lambda i: (0, 0))
    vmem = (2 * tm * CONV_WIDTH * 4 + 2 * CONV_HALO * CONV_WIDTH * 4
            + (SUBLANES * (tm + CONV_HALO) + tm) * CONV_WIDTH * 4 + 2 * tm * CONV_WIDTH * 2
            + 6 * tm * CONV_WIDTH * 4 + (6 << 20))
    return pl.pallas_call(
        _conv_kernel,
        grid=(SEQ // tm,),
        in_specs=[
            pl.BlockSpec((tm, CONV_WIDTH), lambda i: (i, 0)),
            pl.BlockSpec((CONV_HALO, CONV_WIDTH), lambda i: (jnp.maximum(i * halo_blocks - 1, 0), 0)),
            pl.BlockSpec((CONV_KERNEL, CONV_WIDTH), lambda i: (0, 0)),
            vec(CONV_WIDTH), vec(CONV_WIDTH), vec(CONV_WIDTH),
        ],
        out_specs=pl.BlockSpec((tm, CONV_WIDTH), lambda i: (i, 0)),
        out_shape=jax.ShapeDtypeStruct((SEQ, CONV_WIDTH), BF16),
        scratch_shapes=[pltpu.VMEM((SUBLANES, tm + CONV_HALO, CONV_WIDTH), F32),
                        pltpu.VMEM((tm, CONV_WIDTH), F32),
                        pltpu.VMEM((CONV_KERNEL, SUBLANES, CONV_WIDTH), F32)],
        compiler_params=_params(("parallel",), vmem),
        name="convbranch",
    )(v, v, w_dw, b_dw.reshape(1, -1), ln_g.reshape(1, -1), ln_b.reshape(1, -1))


N_SLOPE_PARTS = 3
AUG_LANES = 128
POS_SPLIT = 256
assert POS_SPLIT <= 256 and ATT_TQ // POS_SPLIT <= 256
assert ATT_TQ % ATT_TK == 0 and ATT_TK % ATT_CB == 0 and ATT_CB % CHUNK == 0


def _aug_operand(n_rows, parts, key_side):
    lane = lax.broadcasted_iota(jnp.int32, (n_rows, AUG_LANES), 1)
    r = lax.broadcasted_iota(jnp.int32, (n_rows, AUG_LANES), 0)
    hi = ((r // POS_SPLIT) * POS_SPLIT).astype(F32)
    lo = (r % POS_SPLIT).astype(F32)
    piece = lane % N_SLOPE_PARTS
    part = jnp.where(piece == 0, parts[0], jnp.where(piece == 1, parts[1], parts[2]))
    zero = jnp.zeros((n_rows, AUG_LANES), F32)
    if key_side:
        vals = (-part, -part, hi, lo)
    else:
        vals = (hi, lo, part, part)
    out = zero
    for g, v in enumerate(vals):
        out = jnp.where(lane // N_SLOPE_PARTS == g, v, out)
    return out


def _diagonal_blocks():
    tq, tk, cb = ATT_TQ, ATT_TK, ATT_CB
    nqc = tq // cb
    out = []
    for d in range(tq // tk):
        for b in range(2 * nqc):
            q_lo = (b % nqc) * cb
            if q_lo + cb <= d * tk:
                continue
            out.append((d, b, None if q_lo >= (d + 1) * tk else 1 + (q_lo - d * tk) // cb))
    return out


def _attn_kernel(parts_ref, lam_ref, q_ref, k_ref, vt_ref, g_ref, o_ref,
                 qs_ref, kaug_ref, corr_ref, s_ref, m_ref, l_ref, acc_ref):
    tq, tk, cb = ATT_TQ, ATT_TK, ATT_CB
    nqc = tq // cb
    nb = 2 * nqc
    h = pl.program_id(0)
    i = pl.program_id(1)
    n_before = i * (tq // tk)
    parts = [parts_ref[h * N_SLOPE_PARTS + n] for n in range(N_SLOPE_PARTS)]
    sl = parts[0] + parts[1] + parts[2]

    @pl.when(i == 0)
    def _():
        kaug_ref[...] = _aug_operand(tk, parts, key_side=True).astype(BF16)
        qaug = _aug_operand(tq, parts, key_side=False).astype(BF16)
        qs_ref[0:tq, V_HEAD_DIM:] = qaug
        qs_ref[tq:, V_HEAD_DIM:] = qaug
        corr_ref[0] = jnp.zeros((tk, cb), F32)
        for c in range(tk // cb):
            krel = lax.broadcasted_iota(jnp.int32, (tk, cb), 0)
            qrel = lax.broadcasted_iota(jnp.int32, (tk, cb), 1) + c * cb
            after = jnp.maximum(krel - qrel, 0).astype(F32)
            allowed = (krel // CHUNK) <= (qrel // CHUNK)
            corr_ref[1 + c] = jnp.where(allowed, -2.0 * sl * after, NEG_INF)

    q = q_ref[...]
    lane = lax.broadcasted_iota(jnp.int32, (tq, V_HEAD_DIM), 1)
    zero = jnp.zeros_like(q)
    qs_ref[0:tq, 0:V_HEAD_DIM] = jnp.where(lane < QK_HEAD_DIM, q, zero)
    qs_ref[tq:, 0:V_HEAD_DIM] = jnp.where(lane >= QK_HEAD_DIM, q, zero)

    m_ref[...] = jnp.full_like(m_ref, NEG_INF)
    l_ref[...] = jnp.zeros_like(l_ref)
    acc_ref[...] = jnp.zeros_like(acc_ref)

    def scores(t, b, slot):
        k = k_ref[pl.ds(pl.multiple_of(t * tk, tk), tk), :]
        kx = jnp.concatenate([k, kaug_ref[...]], axis=1)
        s_ref[slot] = lax.dot_general(kx, qs_ref[b * cb:(b + 1) * cb, :], (((1,), (1,)), ((), ())),
                                      preferred_element_type=F32)

    def value_tile(t):
        return vt_ref[:, pl.ds(pl.multiple_of(t * tk, tk), tk)]

    def softmax_values(t, b, slot, corr, vt):
        off = sl * (i * tq - t * tk).astype(F32)
        s = s_ref[slot]
        if corr is not None:
            s = s + corr_ref[corr]
        m_prev = m_ref[b]
        m_new = jnp.maximum(m_prev, jnp.max(s, axis=0, keepdims=True) - off)
        alpha = jnp.exp2(m_prev - m_new)
        p = jnp.exp2(s - (m_new + off))
        l_ref[b] = alpha * l_ref[b] + jnp.sum(p, axis=0, keepdims=True)
        m_ref[b] = m_new
        pv = jnp.dot(vt, p.astype(BF16), preferred_element_type=F32)
        acc_ref[b] = alpha * acc_ref[b] + pv

    blocks = _diagonal_blocks()
    assert [blk[:2] for blk in blocks[:ATT_SKEW_S]] == [(0, b) for b in range(ATT_SKEW_S)]
    assert nb % ATT_RING == 0

    for b in range(ATT_SKEW_S):
        scores(0, b, b % ATT_RING)

    tiles_per_iter = tq // tk

    def body(jj, carry):
        for u in range(tiles_per_iter):
            j = jj * tiles_per_iter + u
            vt = value_tile(j)
            for b in range(nb):
                bs = b + ATT_SKEW_S
                scores(j + bs // nb, bs % nb, bs % ATT_RING)
                softmax_values(j, b, b % ATT_RING, None, vt)
        return carry

    lax.fori_loop(0, i, body, 0)

    vts = {}
    for n, (d, b, corr) in enumerate(blocks):
        if d not in vts:
            vts[d] = value_tile(n_before + d)
        if n + ATT_SKEW_S < len(blocks):
            d2, b2, _ = blocks[n + ATT_SKEW_S]
            scores(n_before + d2, b2, (n + ATT_SKEW_S) % ATT_RING)
        softmax_values(n_before + d, b, n % ATT_RING, corr, vts[d])

    lam = lam_ref[0]
    for c in range(nqc):
        d = acc_ref[c] / l_ref[c] - lam * (acc_ref[nqc + c] / l_ref[nqc + c])
        ms = jnp.mean(d * d, axis=0, keepdims=True)
        y = d * lax.rsqrt(ms + EPS) * g_ref[...] * (1.0 - LAMBDA_INIT)
        o_ref[c * cb:(c + 1) * cb, :] = y.T.astype(BF16)


def _slope_parts():
    rest = (2.0 ** (-8.0 * (np.arange(N_HEADS, dtype=np.float64) + 1.0) / N_HEADS) * LOG2E).astype(np.float32)
    pieces = []
    for _ in range(N_SLOPE_PARTS):
        piece = rest.astype(jnp.bfloat16).astype(np.float32)
        pieces.append(piece)
        rest = rest - piece
    return np.stack(pieces, axis=1).reshape(-1)


def _attention(qk, vt, slope_parts, lam, subln_g):
    tq, tk, cb = ATT_TQ, ATT_TK, ATT_CB
    nh = N_HEADS
    nb = 2 * tq // cb
    smem = pl.BlockSpec(memory_space=pltpu.SMEM)
    scratch = [
        pltpu.VMEM((2 * tq, V_HEAD_DIM + AUG_LANES), BF16),
        pltpu.VMEM((tk, AUG_LANES), BF16),
        pltpu.VMEM((1 + tk // cb, tk, cb), F32),
        pltpu.VMEM((ATT_RING, tk, cb), F32),
        pltpu.VMEM((nb, 1, cb), F32),
        pltpu.VMEM((nb, 1, cb), F32),
        pltpu.VMEM((nb, V_HEAD_DIM, cb), F32),
    ]
    vmem = (2 * 2 * SEQ * V_HEAD_DIM * 2 + 4 * tq * V_HEAD_DIM * 2
            + 2 * tq * (V_HEAD_DIM + AUG_LANES) * 2 + tk * AUG_LANES * 2 + (1 + tk // cb) * tk * cb * 4
            + ATT_RING * tk * cb * 4 + 2 * nb * 8 * cb * 4
            + nb * V_HEAD_DIM * cb * 4 + 8 * tk * cb * 4 + (6 << 20))
    return pl.pallas_call(
        _attn_kernel,
        grid=(nh, SEQ // tq),
        in_specs=[
            smem, smem,
            pl.BlockSpec((tq, V_HEAD_DIM), lambda h, i: (i, h)),
            pl.BlockSpec((SEQ, V_HEAD_DIM), lambda h, i: (0, nh + h)),
            pl.BlockSpec((V_HEAD_DIM, SEQ), lambda h, i: (h, 0)),
            pl.BlockSpec((V_HEAD_DIM, 1), lambda h, i: (0, 0)),
        ],
        out_specs=pl.BlockSpec((tq, V_HEAD_DIM), lambda h, i: (i, h)),
        out_shape=jax.ShapeDtypeStruct((SEQ, ATTN_WIDTH), BF16),
        scratch_shapes=scratch,
        compiler_params=_params(("parallel", "arbitrary"), vmem),
        name="diffattn",
    )(slope_parts, lam, qk, qk, vt, subln_g.reshape(V_HEAD_DIM, 1))


def _outproj_kernel(x_ref, gate_ref, gpost_ref, yc_ref, ya_ref, wc_ref, wa_ref, o_ref, vec_ref):
    y = jnp.dot(yc_ref[...], wc_ref[...], preferred_element_type=F32)
    o_ref[...] = y + jnp.dot(ya_ref[...], wa_ref[...], preferred_element_type=F32)
    _gated_residual(x_ref, o_ref, gpost_ref, gate_ref, 1.0, o_ref, vec_ref)


def _outproj(x, mod, g_post, y_conv, y_attn, w_out):
    tm = OUTPROJ_TM
    vmem = (2 * 2 * tm * D_MODEL * 4 + 2 * 2 * tm * CONV_WIDTH * 2 + 2 * D_MODEL * D_MODEL * 2
            + 3 * tm * D_MODEL * 4 + (6 << 20))
    return pl.pallas_call(
        _outproj_kernel,
        grid=(SEQ // tm,),
        in_specs=[
            pl.BlockSpec((tm, D_MODEL), lambda i: (i, 0)),
            pl.BlockSpec((1, D_MODEL), lambda i: (0, 5)),
            pl.BlockSpec((1, D_MODEL), lambda i: (0, 0)),
            pl.BlockSpec((tm, CONV_WIDTH), lambda i: (i, 0)),
            pl.BlockSpec((tm, ATTN_WIDTH), lambda i: (i, 0)),
            pl.BlockSpec((CONV_WIDTH, D_MODEL), lambda i: (0, 0)),
            pl.BlockSpec((ATTN_WIDTH, D_MODEL), lambda i: (1, 0)),
        ],
        out_specs=pl.BlockSpec((tm, D_MODEL), lambda i: (i, 0)),
        out_shape=jax.ShapeDtypeStruct((SEQ, D_MODEL), F32),
        scratch_shapes=[pltpu.VMEM((1, D_MODEL), F32)],
        compiler_params=_params(("parallel",), vmem),
        name="outproj",
    )(x, mod, g_post.reshape(1, D_MODEL), y_conv, y_attn, w_out, w_out)


def kernel(x, c, w_ada, b_ada, g_pre, g_post, w_ffn1_gate, w_ffn1_up, w_ffn1_down, w_in, b_in_conv, w_dw, b_dw, conv_ln_g, conv_ln_b, lam_q1, lam_k1, lam_q2, lam_k2, subln_g, w_out, w_ffn2_gate, w_ffn2_up, w_ffn2_down):
    bf = lambda w: w[0].astype(BF16)
    x2 = x.reshape(SEQ, D_MODEL)

    mod = _ada(c, w_ada[0], b_ada[0])

    x2 = _ffn(x2, mod, 0, g_pre[0, 0], g_post[0, 0], w_ffn1_gate[0], w_ffn1_up[0], w_ffn1_down[0])

    v_conv, qk, vt = _inproj(x2, mod, g_pre[0, 1], w_in[0], b_in_conv[0])
    y_conv = _conv(v_conv, w_dw[0], b_dw[0], conv_ln_g[0], conv_ln_b[0])

    lam = (jnp.exp(jnp.sum(lam_q1[0].astype(F32) * lam_k1[0].astype(F32)))
           - jnp.exp(jnp.sum(lam_q2[0].astype(F32) * lam_k2[0].astype(F32)))
           + LAMBDA_INIT).reshape(1)
    slope_parts = jnp.asarray(_slope_parts())
    y_attn = _attention(qk, vt, slope_parts, lam, subln_g[0])

    x2 = _outproj(x2, mod, g_post[0, 1], y_conv, y_attn, bf(w_out))

    x2 = _ffn(x2, mod, 6, g_pre[0, 2], g_post[0, 2], w_ffn2_gate[0], w_ffn2_up[0], w_ffn2_down[0])
    return x2.reshape(1, SEQ, D_MODEL)
```

```python
import math

import jax
import jax.numpy as jnp
import numpy as np
from jax import lax
from jax.experimental import pallas as pl
from jax.experimental.pallas import tpu as pltpu

D_MODEL = 2048
SEQ = 8192
CHUNK = 64
CONV_WIDTH = D_MODEL // 2
ATTN_WIDTH = D_MODEL - CONV_WIDTH
CONV_KERNEL = 31
N_HEADS = 8
V_HEAD_DIM = ATTN_WIDTH // N_HEADS
QK_HEAD_DIM = V_HEAD_DIM // 2
D_FF = 5632
N_MOD = 9
IN_WIDTH = 2 * CONV_WIDTH + 3 * ATTN_WIDTH
EPS = 1e-6
NEG_INF = -1e30
LAMBDA_INIT = 0.8 - 0.6 * math.exp(-0.3 * 0)

F32 = jnp.float32
BF16 = jnp.bfloat16

V7X_VMEM_BUDGET_BYTES = 62 * 1024 * 1024

ADA_TN = 1024
FFN_TM = 1024
FFN_TF = 256
INPROJ_TM = 1024
OUTPROJ_TM = 512
PROJ_TN = 1024
CONV_TM = 512
CONV_HALO = 32
CONV_RC = 32
CONV_CC = 512
SUBLANES = 8
ATT_TQ = 2048
ATT_TK = 512
ATT_CB = 256
ATT_SKEW_S = 3
ATT_RING = 4


def _params(semantics, vmem_bytes):
    return pltpu.CompilerParams(
        dimension_semantics=semantics,
        vmem_limit_bytes=min(int(vmem_bytes), V7X_VMEM_BUDGET_BYTES),
    )


ROW_CHUNK = 16


def _norm_rows(src_ref, scale_ref, dst_ref, finish):
    for c in range(src_ref.shape[0] // ROW_CHUNK):
        rows = slice(c * ROW_CHUNK, (c + 1) * ROW_CHUNK)
        x = src_ref[rows, :]
        ms = jnp.mean(x * x, axis=-1, keepdims=True)
        dst_ref[rows, :] = finish(rows, x * lax.rsqrt(ms + EPS) * scale_ref[...]).astype(dst_ref.dtype)


def _modulated_norm(x_ref, gpre_ref, scale_ref, shift_ref, h_ref, vec_ref):
    vec_ref[...] = gpre_ref[...] * (1.0 + scale_ref[...])
    _norm_rows(x_ref, vec_ref, h_ref, lambda rows, y: y + shift_ref[...])


def _gated_residual(x_ref, y_ref, gpost_ref, gate_ref, gate_factor, o_ref, vec_ref):
    vec_ref[...] = gpost_ref[...] * (gate_factor * gate_ref[...])
    _norm_rows(y_ref, vec_ref, o_ref, lambda rows, r: x_ref[rows, :] + r)


def _ada_kernel(c_ref, w_ref, b_ref, o_ref):
    c = c_ref[...]
    s = (c * jax.nn.sigmoid(c)).astype(BF16)
    s8 = jnp.broadcast_to(s, (8, D_MODEL))
    r = jnp.dot(s8, w_ref[...].astype(BF16), preferred_element_type=F32)
    o_ref[...] = r[0:1, :] + b_ref[...]


def _ada(c, w, b):
    n = w.shape[1]
    return pl.pallas_call(
        _ada_kernel,
        grid=(n // ADA_TN,),
        in_specs=[
            pl.BlockSpec((1, D_MODEL), lambda j: (0, 0)),
            pl.BlockSpec((D_MODEL, ADA_TN), lambda j: (0, j)),
            pl.BlockSpec((1, ADA_TN), lambda j: (0, j)),
        ],
        out_specs=pl.BlockSpec((1, ADA_TN), lambda j: (0, j)),
        out_shape=jax.ShapeDtypeStruct((1, n), F32),
        compiler_params=_params(("parallel",), 2 * D_MODEL * ADA_TN * 4 + D_MODEL * ADA_TN * 2 + (8 << 20)),
        name="adaln",
    )(c, w, b.reshape(1, n))


def _ffn_kernel(x_ref, shift_ref, scale_ref, gate_ref, gpre_ref, gpost_ref,
                wg_ref, wu_ref, wd_ref, o_ref, h_ref, vec_ref):
    j = pl.program_id(1)
    last = pl.num_programs(1) - 1

    def down_projection():
        h = h_ref[...]
        g = jnp.dot(h, wg_ref[...].astype(BF16), preferred_element_type=F32)
        u = jnp.dot(h, wu_ref[...].astype(BF16), preferred_element_type=F32)
        a = (g * jax.nn.sigmoid(g) * u).astype(BF16)
        return jnp.dot(a, wd_ref[...].astype(BF16), preferred_element_type=F32)

    @pl.when(j == 0)
    def _():
        _modulated_norm(x_ref, gpre_ref, scale_ref, shift_ref, h_ref, vec_ref)
        o_ref[...] = down_projection()

    @pl.when(jnp.logical_and(j > 0, j < last))
    def _():
        o_ref[...] += down_projection()

    @pl.when(j == last)
    def _():
        o_ref[...] += down_projection()
        _gated_residual(x_ref, o_ref, gpost_ref, gate_ref, 0.5, o_ref, vec_ref)


def _ffn(x, mod, mod_base, g_pre, g_post, wg, wu, wd):
    tm, tf = FFN_TM, FFN_TF
    row = lambda k: pl.BlockSpec((1, D_MODEL), lambda i, j, k=k: (0, k))
    vec = pl.BlockSpec((1, D_MODEL), lambda i, j: (0, 0))
    vmem = (2 * 2 * tm * D_MODEL * 4
            + tm * D_MODEL * 2
            + 2 * 3 * D_MODEL * tf * 4
            + 3 * D_MODEL * tf * 2
            + 3 * tm * tf * 4
            + (4 << 20))
    return pl.pallas_call(
        _ffn_kernel,
        grid=(SEQ // tm, D_FF // tf),
        in_specs=[
            pl.BlockSpec((tm, D_MODEL), lambda i, j: (i, 0)),
            row(mod_base), row(mod_base + 1), row(mod_base + 2),
            vec, vec,
            pl.BlockSpec((D_MODEL, tf), lambda i, j: (0, j)),
            pl.BlockSpec((D_MODEL, tf), lambda i, j: (0, j)),
            pl.BlockSpec((tf, D_MODEL), lambda i, j: (j, 0)),
        ],
        out_specs=pl.BlockSpec((tm, D_MODEL), lambda i, j: (i, 0)),
        out_shape=jax.ShapeDtypeStruct((SEQ, D_MODEL), F32),
        scratch_shapes=[pltpu.VMEM((tm, D_MODEL), BF16), pltpu.VMEM((1, D_MODEL), F32)],
        compiler_params=_params(("parallel", "arbitrary"), vmem),
        name="ffn",
    )(x, mod, mod, mod, g_pre.reshape(1, D_MODEL), g_post.reshape(1, D_MODEL), wg, wu, wd)


N_U_BLOCKS = 2 * CONV_WIDTH // PROJ_TN
assert N_U_BLOCKS == 2
N_QK_BLOCKS = 2 * ATTN_WIDTH // PROJ_TN
assert PROJ_TN == ATTN_WIDTH
LOG2E = math.log2(math.e)
Q_SCALE_LOG2 = QK_HEAD_DIM ** -0.5 * LOG2E


def _inproj_kernel(x_ref, shift_ref, scale_ref, gpre_ref, w_ref, bin_ref, glu_ref, qk_ref, vt_ref,
                   h_ref, vec_ref, ua_ref):
    j = pl.program_id(1)

    @pl.when(j == 0)
    def _():
        _modulated_norm(x_ref, gpre_ref, scale_ref, shift_ref, h_ref, vec_ref)
        ua_ref[...] = jnp.dot(h_ref[...], w_ref[...].astype(BF16), preferred_element_type=F32)

    @pl.when(j == 1)
    def _():
        ug = jnp.dot(h_ref[...], w_ref[...].astype(BF16), preferred_element_type=F32)
        a = ua_ref[...] + bin_ref[:, 0:CONV_WIDTH]
        glu_ref[...] = a * jax.nn.sigmoid(ug + bin_ref[:, CONV_WIDTH:])

    @pl.when(jnp.logical_and(j >= N_U_BLOCKS, j < N_U_BLOCKS + N_QK_BLOCKS))
    def _():
        c = jnp.where(j == N_U_BLOCKS, Q_SCALE_LOG2, 1.0)
        qk_ref[...] = (jnp.dot(h_ref[...], w_ref[...].astype(BF16), preferred_element_type=F32) * c).astype(BF16)

    @pl.when(j == N_U_BLOCKS + N_QK_BLOCKS)
    def _():
        v = jnp.dot(h_ref[...], w_ref[...].astype(BF16), preferred_element_type=F32)
        vt_ref[...] = v.T.astype(BF16)


def _inproj(x, mod, g_pre, w_in, b_in):
    tm, tn = INPROJ_TM, PROJ_TN
    n_main = N_U_BLOCKS + N_QK_BLOCKS
    vmem = (2 * tm * D_MODEL * 4 + tm * D_MODEL * 2 + 2 * D_MODEL * tn * 4 + D_MODEL * tn * 2
            + 2 * tm * tn * (4 + 2) + 2 * ATTN_WIDTH * tm * 2 + 3 * tm * tn * 4 + (4 << 20))
    return pl.pallas_call(
        _inproj_kernel,
        grid=(SEQ // tm, n_main + 1),
        in_specs=[
            pl.BlockSpec((tm, D_MODEL), lambda i, j: (i, 0)),
            pl.BlockSpec((1, D_MODEL), lambda i, j: (0, 3)),
            pl.BlockSpec((1, D_MODEL), lambda i, j: (0, 4)),
            pl.BlockSpec((1, D_MODEL), lambda i, j: (0, 0)),
            pl.BlockSpec((D_MODEL, tn), lambda i, j: (0, j)),
            pl.BlockSpec((1, 2 * CONV_WIDTH), lambda i, j: (0, 0)),
        ],
        out_specs=[
            pl.BlockSpec((tm, CONV_WIDTH), lambda i, j: (i, 0)),
            pl.BlockSpec((tm, tn), lambda i, j: (i, jnp.clip(j - N_U_BLOCKS, 0, N_QK_BLOCKS - 1))),
            pl.BlockSpec((ATTN_WIDTH, tm), lambda i, j: (0, i)),
        ],
        out_shape=[
            jax.ShapeDtypeStruct((SEQ, CONV_WIDTH), F32),
            jax.ShapeDtypeStruct((SEQ, 2 * ATTN_WIDTH), BF16),
            jax.ShapeDtypeStruct((ATTN_WIDTH, SEQ), BF16),
        ],
        scratch_shapes=[pltpu.VMEM((tm, D_MODEL), BF16), pltpu.VMEM((1, D_MODEL), F32),
                        pltpu.VMEM((tm, CONV_WIDTH), F32)],
        compiler_params=_params(("parallel", "arbitrary"), vmem),
        name="inproj",
    )(x, mod, mod, g_pre.reshape(1, D_MODEL), w_in, b_in.reshape(1, -1))


def _conv_kernel(v_ref, vprev_ref, wdw_ref, bdw_ref, lng_ref, lnb_ref, o_ref,
                 vbuf_ref, conv_ref, wb_ref):
    i = pl.program_id(0)
    vbuf_ref[0, 0:CONV_HALO, :] = jnp.where(i > 0, vprev_ref[...], 0.0)
    vbuf_ref[0, CONV_HALO:, :] = v_ref[...]
    n_shift = CONV_TM + CONV_HALO - SUBLANES
    for s in range(1, SUBLANES):
        vbuf_ref[s, 0:n_shift, :] = vbuf_ref[0, s:s + n_shift, :]

    for j in range(CONV_KERNEL):
        wb_ref[j] = jnp.broadcast_to(wdw_ref[j:j + 1, :], (SUBLANES, CONV_WIDTH))

    lead = CONV_HALO - (CONV_KERNEL - 1)
    for cb in range(CONV_WIDTH // CONV_CC):
        cs = slice(cb * CONV_CC, (cb + 1) * CONV_CC)
        for r in range(CONV_TM // CONV_RC):
            accs = [jnp.zeros((SUBLANES, CONV_CC), F32) for _ in range(CONV_RC // SUBLANES)]
            for j in range(CONV_KERNEL):
                s, lo = (lead + j) % SUBLANES, r * CONV_RC + (lead + j) // SUBLANES * SUBLANES
                w = wb_ref[j, :, cs]
                for q in range(len(accs)):
                    accs[q] = accs[q] + vbuf_ref[s, lo + q * SUBLANES:lo + (q + 1) * SUBLANES, cs] * w
            for q, acc in enumerate(accs):
                conv_ref[r * CONV_RC + q * SUBLANES:r * CONV_RC + (q + 1) * SUBLANES, cs] = acc

    v = conv_ref[...] + bdw_ref[...]
    mu = jnp.mean(v, axis=-1, keepdims=True)
    xc = v - mu
    var = jnp.mean(xc * xc, axis=-1, keepdims=True)
    y = xc * lax.rsqrt(var + EPS) * lng_ref[...] + lnb_ref[...]
    o_ref[...] = (y * jax.nn.sigmoid(y)).astype(BF16)


def _conv(v, w_dw, b_dw, ln_g, ln_b):
    tm = CONV_TM
    halo_blocks = tm // CONV_HALO
    vec = lambda n: pl.BlockSpec((1, n), lambda i: (0, 0))
    vmem = (2 * tm * CONV_WIDTH * 4 + 2 * CONV_HALO * CONV_WIDTH * 4
            + (SUBLANES * (tm + CONV_HALO) + tm) * CONV_WIDTH * 4 + 2 * tm * CONV_WIDTH * 2
            + 6 * tm * CONV_WIDTH * 4 + (6 << 20))
    return pl.pallas_call(
        _conv_kernel,
        grid=(SEQ // tm,),
        in_specs=[
            pl.BlockSpec((tm, CONV_WIDTH), lambda i: (i, 0)),
            pl.BlockSpec((CONV_HALO, CONV_WIDTH), lambda i: (jnp.maximum(i * halo_blocks - 1, 0), 0)),
            pl.BlockSpec((CONV_KERNEL, CONV_WIDTH), lambda i: (0, 0)),
            vec(CONV_WIDTH), vec(CONV_WIDTH), vec(CONV_WIDTH),
        ],
        out_specs=pl.BlockSpec((tm, CONV_WIDTH), lambda i: (i, 0)),
        out_shape=jax.ShapeDtypeStruct((SEQ, CONV_WIDTH), BF16),
        scratch_shapes=[pltpu.VMEM((SUBLANES, tm + CONV_HALO, CONV_WIDTH), F32),
                        pltpu.VMEM((tm, CONV_WIDTH), F32),
                        pltpu.VMEM((CONV_KERNEL, SUBLANES, CONV_WIDTH), F32)],
        compiler_params=_params(("parallel",), vmem),
        name="convbranch",
    )(v, v, w_dw, b_dw.reshape(1, -1), ln_g.reshape(1, -1), ln_b.reshape(1, -1))


N_SLOPE_PARTS = 3
AUG_LANES = 128
POS_SPLIT = 256
assert POS_SPLIT <= 256 and ATT_TQ // POS_SPLIT <= 256
assert ATT_TQ % ATT_TK == 0 and ATT_TK % ATT_CB == 0 and ATT_CB % CHUNK == 0


def _aug_operand(n_rows, parts, key_side):
    lane = lax.broadcasted_iota(jnp.int32, (n_rows, AUG_LANES), 1)
    r = lax.broadcasted_iota(jnp.int32, (n_rows, AUG_LANES), 0)
    hi = ((r // POS_SPLIT) * POS_SPLIT).astype(F32)
    lo = (r % POS_SPLIT).astype(F32)
    piece = lane % N_SLOPE_PARTS
    part = jnp.where(piece == 0, parts[0], jnp.where(piece == 1, parts[1], parts[2]))
    zero = jnp.zeros((n_rows, AUG_LANES), F32)
    if key_side:
        vals = (-part, -part, hi, lo)
    else:
        vals = (hi, lo, part, part)
    out = zero
    for g, v in enumerate(vals):
        out = jnp.where(lane // N_SLOPE_PARTS == g, v, out)
    return out


def _diagonal_blocks():
    tq, tk, cb = ATT_TQ, ATT_TK, ATT_CB
    nqc = tq // cb
    out = []
    for d in range(tq // tk):
        for b in range(2 * nqc):
            q_lo = (b % nqc) * cb
            if q_lo + cb <= d * tk:
                continue
            out.append((d, b, None if q_lo >= (d + 1) * tk else 1 + (q_lo - d * tk) // cb))
    return out


def _attn_kernel(parts_ref, lam_ref, q_ref, k_ref, vt_ref, g_ref, o_ref,
                 qs_ref, kaug_ref, corr_ref, s_ref, m_ref, l_ref, acc_ref):
    tq, tk, cb = ATT_TQ, ATT_TK, ATT_CB
    nqc = tq // cb
    nb = 2 * nqc
    h = pl.program_id(0)
    i = pl.program_id(1)
    n_before = i * (tq // tk)
    parts = [parts_ref[h * N_SLOPE_PARTS + n] for n in range(N_SLOPE_PARTS)]
    sl = parts[0] + parts[1] + parts[2]

    @pl.when(i == 0)
    def _():
        kaug_ref[...] = _aug_operand(tk, parts, key_side=True).astype(BF16)
        qaug = _aug_operand(tq, parts, key_side=False).astype(BF16)
        qs_ref[0:tq, V_HEAD_DIM:] = qaug
        qs_ref[tq:, V_HEAD_DIM:] = qaug
        corr_ref[0] = jnp.zeros((tk, cb), F32)
        for c in range(tk // cb):
            krel = lax.broadcasted_iota(jnp.int32, (tk, cb), 0)
            qrel = lax.broadcasted_iota(jnp.int32, (tk, cb), 1) + c * cb
            after = jnp.maximum(krel - qrel, 0).astype(F32)
            allowed = (krel // CHUNK) <= (qrel // CHUNK)
            corr_ref[1 + c] = jnp.where(allowed, -2.0 * sl * after, NEG_INF)

    q = q_ref[...]
    lane = lax.broadcasted_iota(jnp.int32, (tq, V_HEAD_DIM), 1)
    zero = jnp.zeros_like(q)
    qs_ref[0:tq, 0:V_HEAD_DIM] = jnp.where(lane < QK_HEAD_DIM, q, zero)
    qs_ref[tq:, 0:V_HEAD_DIM] = jnp.where(lane >= QK_HEAD_DIM, q, zero)

    m_ref[...] = jnp.full_like(m_ref, NEG_INF)
    l_ref[...] = jnp.zeros_like(l_ref)
    acc_ref[...] = jnp.zeros_like(acc_ref)

    def scores(t, b, slot):
        k = k_ref[pl.ds(pl.multiple_of(t * tk, tk), tk), :]
        kx = jnp.concatenate([k, kaug_ref[...]], axis=1)
        s_ref[slot] = lax.dot_general(kx, qs_ref[b * cb:(b + 1) * cb, :], (((1,), (1,)), ((), ())),
                                      preferred_element_type=F32)

    def value_tile(t):
        return vt_ref[:, pl.ds(pl.multiple_of(t * tk, tk), tk)]

    def softmax_values(t, b, slot, corr, vt):
        off = sl * (i * tq - t * tk).astype(F32)
        s = s_ref[slot]
        if corr is not None:
            s = s + corr_ref[corr]
        m_prev = m_ref[b]
        m_new = jnp.maximum(m_prev, jnp.max(s, axis=0, keepdims=True) - off)
        alpha = jnp.exp2(m_prev - m_new)
        p = jnp.exp2(s - (m_new + off))
        l_ref[b] = alpha * l_ref[b] + jnp.sum(p, axis=0, keepdims=True)
        m_ref[b] = m_new
        pv = jnp.dot(vt, p.astype(BF16), preferred_element_type=F32)
        acc_ref[b] = alpha * acc_ref[b] + pv

    blocks = _diagonal_blocks()
    assert [blk[:2] for blk in blocks[:ATT_SKEW_S]] == [(0, b) for b in range(ATT_SKEW_S)]
    assert nb % ATT_RING == 0

    for b in range(ATT_SKEW_S):
        scores(0, b, b % ATT_RING)

    tiles_per_iter = tq // tk

    def body(jj, carry):
        for u in range(tiles_per_iter):
            j = jj * tiles_per_iter + u
            vt = value_tile(j)
            for b in range(nb):
                bs = b + ATT_SKEW_S
                scores(j + bs // nb, bs % nb, bs % ATT_RING)
                softmax_values(j, b, b % ATT_RING, None, vt)
        return carry

    lax.fori_loop(0, i, body, 0)

    vts = {}
    for n, (d, b, corr) in enumerate(blocks):
        if d not in vts:
            vts[d] = value_tile(n_before + d)
        if n + ATT_SKEW_S < len(blocks):
            d2, b2, _ = blocks[n + ATT_SKEW_S]
            scores(n_before + d2, b2, (n + ATT_SKEW_S) % ATT_RING)
        softmax_values(n_before + d, b, n % ATT_RING, corr, vts[d])

    lam = lam_ref[0]
    for c in range(nqc):
        d = acc_ref[c] / l_ref[c] - lam * (acc_ref[nqc + c] / l_ref[nqc + c])
        ms = jnp.mean(d * d, axis=0, keepdims=True)
        y = d * lax.rsqrt(ms + EPS) * g_ref[...] * (1.0 - LAMBDA_INIT)
        o_ref[c * cb:(c + 1) * cb, :] = y.T.astype(BF16)


def _slope_parts():
    rest = (2.0 ** (-8.0 * (np.arange(N_HEADS, dtype=np.float64) + 1.0) / N_HEADS) * LOG2E).astype(np.float32)
    pieces = []
    for _ in range(N_SLOPE_PARTS):
        piece = rest.astype(jnp.bfloat16).astype(np.float32)
        pieces.append(piece)
        rest = rest - piece
    return np.stack(pieces, axis=1).reshape(-1)


def _attention(qk, vt, slope_parts, lam, subln_g):
    tq, tk, cb = ATT_TQ, ATT_TK, ATT_CB
    nh = N_HEADS
    nb = 2 * tq // cb
    smem = pl.BlockSpec(memory_space=pltpu.SMEM)
    scratch = [
        pltpu.VMEM((2 * tq, V_HEAD_DIM + AUG_LANES), BF16),
        pltpu.VMEM((tk, AUG_LANES), BF16),
        pltpu.VMEM((1 + tk // cb, tk, cb), F32),
        pltpu.VMEM((ATT_RING, tk, cb), F32),
        pltpu.VMEM((nb, 1, cb), F32),
        pltpu.VMEM((nb, 1, cb), F32),
        pltpu.VMEM((nb, V_HEAD_DIM, cb), F32),
    ]
    vmem = (2 * 2 * SEQ * V_HEAD_DIM * 2 + 4 * tq * V_HEAD_DIM * 2
            + 2 * tq * (V_HEAD_DIM + AUG_LANES) * 2 + tk * AUG_LANES * 2 + (1 + tk // cb) * tk * cb * 4
            + ATT_RING * tk * cb * 4 + 2 * nb * 8 * cb * 4
            + nb * V_HEAD_DIM * cb * 4 + 8 * tk * cb * 4 + (6 << 20))
    return pl.pallas_call(
        _attn_kernel,
        grid=(nh, SEQ // tq),
        in_specs=[
            smem, smem,
            pl.BlockSpec((tq, V_HEAD_DIM), lambda h, i: (i, h)),
            pl.BlockSpec((SEQ, V_HEAD_DIM), lambda h, i: (0, nh + h)),
            pl.BlockSpec((V_HEAD_DIM, SEQ), lambda h, i: (h, 0)),
            pl.BlockSpec((V_HEAD_DIM, 1), lambda h, i: (0, 0)),
        ],
        out_specs=pl.BlockSpec((tq, V_HEAD_DIM), lambda h, i: (i, h)),
        out_shape=jax.ShapeDtypeStruct((SEQ, ATTN_WIDTH), BF16),
        scratch_shapes=scratch,
        compiler_params=_params(("parallel", "arbitrary"), vmem),
        name="diffattn",
    )(slope_parts, lam, qk, qk, vt, subln_g.reshape(V_HEAD_DIM, 1))


def _outproj_kernel(x_ref, gate_ref, gpost_ref, yc_ref, ya_ref, wc_ref, wa_ref, o_ref, vec_ref):
    y = jnp.dot(yc_ref[...], wc_ref[...], preferred_element_type=F32)
    o_ref[...] = y + jnp.dot(ya_ref[...], wa_ref[...], preferred_element_type=F32)
    _gated_residual(x_ref, o_ref, gpost_ref, gate_ref, 1.0, o_ref, vec_ref)


def _outproj(x, mod, g_post, y_conv, y_attn, w_out):
    tm = OUTPROJ_TM
    vmem = (2 * 2 * tm * D_MODEL * 4 + 2 * 2 * tm * CONV_WIDTH * 2 + 2 * D_MODEL * D_MODEL * 2
            + 3 * tm * D_MODEL * 4 + (6 << 20))
    return pl.pallas_call(
        _outproj_kernel,
        grid=(SEQ // tm,),
        in_specs=[
            pl.BlockSpec((tm, D_MODEL), lambda i: (i, 0)),
            pl.BlockSpec((1, D_MODEL), lambda i: (0, 5)),
            pl.BlockSpec((1, D_MODEL), lambda i: (0, 0)),
            pl.BlockSpec((tm, CONV_WIDTH), lambda i: (i, 0)),
            pl.BlockSpec((tm, ATTN_WIDTH), lambda i: (i, 0)),
            pl.BlockSpec((CONV_WIDTH, D_MODEL), lambda i: (0, 0)),
            pl.BlockSpec((ATTN_WIDTH, D_MODEL), lambda i: (1, 0)),
        ],
        out_specs=pl.BlockSpec((tm, D_MODEL), lambda i: (i, 0)),
        out_shape=jax.ShapeDtypeStruct((SEQ, D_MODEL), F32),
        scratch_shapes=[pltpu.VMEM((1, D_MODEL), F32)],
        compiler_params=_params(("parallel",), vmem),
        name="outproj",
    )(x, mod, g_post.reshape(1, D_MODEL), y_conv, y_attn, w_out, w_out)


def kernel(x, c, w_ada, b_ada, g_pre, g_post, w_ffn1_gate, w_ffn1_up, w_ffn1_down, w_in, b_in_conv, w_dw, b_dw, conv_ln_g, conv_ln_b, lam_q1, lam_k1, lam_q2, lam_k2, subln_g, w_out, w_ffn2_gate, w_ffn2_up, w_ffn2_down):
    bf = lambda w: w[0].astype(BF16)
    x2 = x.reshape(SEQ, D_MODEL)

    mod = _ada(c, w_ada[0], b_ada[0])

    x2 = _ffn(x2, mod, 0, g_pre[0, 0], g_post[0, 0], w_ffn1_gate[0], w_ffn1_up[0], w_ffn1_down[0])

    v_conv, qk, vt = _inproj(x2, mod, g_pre[0, 1], w_in[0], b_in_conv[0])
    y_conv = _conv(v_conv, w_dw[0], b_dw[0], conv_ln_g[0], conv_ln_b[0])

    lam = (jnp.exp(jnp.sum(lam_q1[0].astype(F32) * lam_k1[0].astype(F32)))
           - jnp.exp(jnp.sum(lam_q2[0].astype(F32) * lam_k2[0].astype(F32)))
           + LAMBDA_INIT).reshape(1)
    slope_parts = jnp.asarray(_slope_parts())
    y_attn = _attention(qk, vt, slope_parts, lam, subln_g[0])

    x2 = _outproj(x2, mod, g_post[0, 1], y_conv, y_attn, bf(w_out))

    x2 = _ffn(x2, mod, 6, g_pre[0, 2], g_post[0, 2], w_ffn2_gate[0], w_ffn2_up[0], w_ffn2_down[0])
    return x2.reshape(1, SEQ, D_MODEL)
```

```python
import math

import jax
import jax.numpy as jnp
import numpy as np
from jax import lax
from jax.experimental import pallas as pl
from jax.experimental.pallas import tpu as pltpu

D_MODEL = 2048
SEQ = 8192
CHUNK = 64
CONV_WIDTH = D_MODEL // 2
ATTN_WIDTH = D_MODEL - CONV_WIDTH
CONV_KERNEL = 31
N_HEADS = 8
V_HEAD_DIM = ATTN_WIDTH // N_HEADS
QK_HEAD_DIM = V_HEAD_DIM // 2
D_FF = 5632
N_MOD = 9
IN_WIDTH = 2 * CONV_WIDTH + 3 * ATTN_WIDTH
EPS = 1e-6
NEG_INF = -1e30
LAMBDA_INIT = 0.8 - 0.6 * math.exp(-0.3 * 0)

F32 = jnp.float32
BF16 = jnp.bfloat16

V7X_VMEM_BUDGET_BYTES = 62 * 1024 * 1024

ADA_TN = 1024
N_MOD_EARLY = 5
ADA_EARLY_COLS = N_MOD_EARLY * D_MODEL
FFN_TM = 1024
FFN_TF = 256
INPROJ_TM = 1024
OUTPROJ_TM = 512
PROJ_TN = 1024
CONV_TM = 512
CONV_HALO = 32
CONV_RC = 32
CONV_CC = 512
SUBLANES = 8
ATT_TQ = 2048
ATT_TK = 512
ATT_CB = 256
ATT_SKEW_S = 3
ATT_RING = 4


def _params(semantics, vmem_bytes):
    return pltpu.CompilerParams(
        dimension_semantics=semantics,
        vmem_limit_bytes=min(int(vmem_bytes), V7X_VMEM_BUDGET_BYTES),
    )


ROW_CHUNK = 16


def _norm_rows(src_ref, scale_ref, dst_ref, finish):
    for c in range(src_ref.shape[0] // ROW_CHUNK):
        rows = slice(c * ROW_CHUNK, (c + 1) * ROW_CHUNK)
        x = src_ref[rows, :]
        ms = jnp.mean(x * x, axis=-1, keepdims=True)
        dst_ref[rows, :] = finish(rows, x * lax.rsqrt(ms + EPS) * scale_ref[...]).astype(dst_ref.dtype)


def _modulated_norm(x_ref, gpre_ref, scale_ref, shift_ref, h_ref, vec_ref):
    vec_ref[...] = gpre_ref[...] * (1.0 + scale_ref[...])
    _norm_rows(x_ref, vec_ref, h_ref, lambda rows, y: y + shift_ref[...])


def _gated_residual(x_ref, y_ref, gpost_ref, gate_ref, gate_factor, o_ref, vec_ref):
    vec_ref[...] = gpost_ref[...] * (gate_factor * gate_ref[...])
    _norm_rows(y_ref, vec_ref, o_ref, lambda rows, r: x_ref[rows, :] + r)


def _ada_columns(c_ref, w_ref, b_ref, o_ref):
    c = c_ref[...]
    s = (c * jax.nn.sigmoid(c)).astype(BF16)
    s8 = jnp.broadcast_to(s, (8, D_MODEL))
    r = jnp.dot(s8, w_ref[...].astype(BF16), preferred_element_type=F32)
    o_ref[...] = r[0:1, :] + b_ref[...]


def _ada_kernel(c_ref, w_ref, b_ref, o_ref):
    _ada_columns(c_ref, w_ref, b_ref, o_ref)


def _ada(c, w, b, n):
    return pl.pallas_call(
        _ada_kernel,
        grid=(n // ADA_TN,),
        in_specs=[
            pl.BlockSpec((1, D_MODEL), lambda j: (0, 0)),
            pl.BlockSpec((D_MODEL, ADA_TN), lambda j: (0, j)),
            pl.BlockSpec((1, ADA_TN), lambda j: (0, j)),
        ],
        out_specs=pl.BlockSpec((1, ADA_TN), lambda j: (0, j)),
        out_shape=jax.ShapeDtypeStruct((1, n), F32),
        compiler_params=_params(("parallel",), 2 * D_MODEL * ADA_TN * 4 + D_MODEL * ADA_TN * 2 + (8 << 20)),
        name="adaln",
    )(c, w, b.reshape(1, -1))


def _ffn_kernel(x_ref, shift_ref, scale_ref, gate_ref, gpre_ref, gpost_ref,
                wg_ref, wu_ref, wd_ref, o_ref, h_ref, vec_ref):
    j = pl.program_id(1)
    last = pl.num_programs(1) - 1

    def down_projection():
        h = h_ref[...]
        g = jnp.dot(h, wg_ref[...].astype(BF16), preferred_element_type=F32)
        u = jnp.dot(h, wu_ref[...].astype(BF16), preferred_element_type=F32)
        a = (g * jax.nn.sigmoid(g) * u).astype(BF16)
        return jnp.dot(a, wd_ref[...].astype(BF16), preferred_element_type=F32)

    @pl.when(j == 0)
    def _():
        _modulated_norm(x_ref, gpre_ref, scale_ref, shift_ref, h_ref, vec_ref)
        o_ref[...] = down_projection()

    @pl.when(jnp.logical_and(j > 0, j < last))
    def _():
        o_ref[...] += down_projection()

    @pl.when(j == last)
    def _():
        o_ref[...] += down_projection()
        _gated_residual(x_ref, o_ref, gpost_ref, gate_ref, 0.5, o_ref, vec_ref)


def _ffn(x, mod, mod_base, g_pre, g_post, wg, wu, wd):
    tm, tf = FFN_TM, FFN_TF
    row = lambda k: pl.BlockSpec((1, D_MODEL), lambda i, j, k=k: (0, k))
    vec = pl.BlockSpec((1, D_MODEL), lambda i, j: (0, 0))
    vmem = (2 * 2 * tm * D_MODEL * 4
            + tm * D_MODEL * 2
            + 2 * 3 * D_MODEL * tf * 4
            + 3 * D_MODEL * tf * 2
            + 3 * tm * tf * 4
            + (4 << 20))
    return pl.pallas_call(
        _ffn_kernel,
        grid=(SEQ // tm, D_FF // tf),
        in_specs=[
            pl.BlockSpec((tm, D_MODEL), lambda i, j: (i, 0)),
            row(mod_base), row(mod_base + 1), row(mod_base + 2),
            vec, vec,
            pl.BlockSpec((D_MODEL, tf), lambda i, j: (0, j)),
            pl.BlockSpec((D_MODEL, tf), lambda i, j: (0, j)),
            pl.BlockSpec((tf, D_MODEL), lambda i, j: (j, 0)),
        ],
        out_specs=pl.BlockSpec((tm, D_MODEL), lambda i, j: (i, 0)),
        out_shape=jax.ShapeDtypeStruct((SEQ, D_MODEL), F32),
        scratch_shapes=[pltpu.VMEM((tm, D_MODEL), BF16), pltpu.VMEM((1, D_MODEL), F32)],
        compiler_params=_params(("parallel", "arbitrary"), vmem),
        name="ffn",
    )(x, mod, mod, mod, g_pre.reshape(1, D_MODEL), g_post.reshape(1, D_MODEL), wg, wu, wd)


N_U_BLOCKS = 2 * CONV_WIDTH // PROJ_TN
assert N_U_BLOCKS == 2
N_QK_BLOCKS = 2 * ATTN_WIDTH // PROJ_TN
assert PROJ_TN == ATTN_WIDTH
LOG2E = math.log2(math.e)
Q_SCALE_LOG2 = QK_HEAD_DIM ** -0.5 * LOG2E


def _inproj_kernel(x_ref, shift_ref, scale_ref, gpre_ref, w_ref, bin_ref, glu_ref, qk_ref, vt_ref,
                   h_ref, vec_ref, ua_ref):
    j = pl.program_id(1)

    @pl.when(j == 0)
    def _():
        _modulated_norm(x_ref, gpre_ref, scale_ref, shift_ref, h_ref, vec_ref)
        ua_ref[...] = jnp.dot(h_ref[...], w_ref[...].astype(BF16), preferred_element_type=F32)

    @pl.when(j == 1)
    def _():
        ug = jnp.dot(h_ref[...], w_ref[...].astype(BF16), preferred_element_type=F32)
        a = ua_ref[...] + bin_ref[:, 0:CONV_WIDTH]
        glu_ref[...] = a * jax.nn.sigmoid(ug + bin_ref[:, CONV_WIDTH:])

    @pl.when(jnp.logical_and(j >= N_U_BLOCKS, j < N_U_BLOCKS + N_QK_BLOCKS))
    def _():
        c = jnp.where(j == N_U_BLOCKS, Q_SCALE_LOG2, 1.0)
        qk_ref[...] = (jnp.dot(h_ref[...], w_ref[...].astype(BF16), preferred_element_type=F32) * c).astype(BF16)

    @pl.when(j == N_U_BLOCKS + N_QK_BLOCKS)
    def _():
        v = jnp.dot(h_ref[...], w_ref[...].astype(BF16), preferred_element_type=F32)
        vt_ref[...] = v.T.astype(BF16)


def _inproj(x, mod, g_pre, w_in, b_in):
    tm, tn = INPROJ_TM, PROJ_TN
    n_main = N_U_BLOCKS + N_QK_BLOCKS
    vmem = (2 * tm * D_MODEL * 4 + tm * D_MODEL * 2 + 2 * D_MODEL * tn * 4 + D_MODEL * tn * 2
            + 2 * tm * tn * (4 + 2) + 2 * ATTN_WIDTH * tm * 2 + 3 * tm * tn * 4 + (4 << 20))
    return pl.pallas_call(
        _inproj_kernel,
        grid=(SEQ // tm, n_main + 1),
        in_specs=[
            pl.BlockSpec((tm, D_MODEL), lambda i, j: (i, 0)),
            pl.BlockSpec((1, D_MODEL), lambda i, j: (0, 3)),
            pl.BlockSpec((1, D_MODEL), lambda i, j: (0, 4)),
            pl.BlockSpec((1, D_MODEL), lambda i, j: (0, 0)),
            pl.BlockSpec((D_MODEL, tn), lambda i, j: (0, j)),
            pl.BlockSpec((1, 2 * CONV_WIDTH), lambda i, j: (0, 0)),
        ],
        out_specs=[
            pl.BlockSpec((tm, CONV_WIDTH), lambda i, j: (i, 0)),
            pl.BlockSpec((tm, tn), lambda i, j: (i, jnp.clip(j - N_U_BLOCKS, 0, N_QK_BLOCKS - 1))),
            pl.BlockSpec((ATTN_WIDTH, tm), lambda i, j: (0, i)),
        ],
        out_shape=[
            jax.ShapeDtypeStruct((SEQ, CONV_WIDTH), F32),
            jax.ShapeDtypeStruct((SEQ, 2 * ATTN_WIDTH), BF16),
            jax.ShapeDtypeStruct((ATTN_WIDTH, SEQ), BF16),
        ],
        scratch_shapes=[pltpu.VMEM((tm, D_MODEL), BF16), pltpu.VMEM((1, D_MODEL), F32),
                        pltpu.VMEM((tm, CONV_WIDTH), F32)],
        compiler_params=_params(("parallel", "arbitrary"), vmem),
        name="inproj",
    )(x, mod, mod, g_pre.reshape(1, D_MODEL), w_in, b_in.reshape(1, -1))


def _conv_kernel(v_ref, vprev_ref, wdw_ref, bdw_ref, lng_ref, lnb_ref, c_ref, wada_ref, bada_ref,
                 o_ref, mod_ref, vbuf_ref, conv_ref, wb_ref):
    i = pl.program_id(0)
    _ada_columns(c_ref, wada_ref, bada_ref, mod_ref)
    vbuf_ref[0, 0:CONV_HALO, :] = jnp.where(i > 0, vprev_ref[...], 0.0)
    vbuf_ref[0, CONV_HALO:, :] = v_ref[...]
    n_shift = CONV_TM + CONV_HALO - SUBLANES
    for s in range(1, SUBLANES):
        vbuf_ref[s, 0:n_shift, :] = vbuf_ref[0, s:s + n_shift, :]

    for j in range(CONV_KERNEL):
        wb_ref[j] = jnp.broadcast_to(wdw_ref[j:j + 1, :], (SUBLANES, CONV_WIDTH))

    lead = CONV_HALO - (CONV_KERNEL - 1)
    for cb in range(CONV_WIDTH // CONV_CC):
        cs = slice(cb * CONV_CC, (cb + 1) * CONV_CC)
        for r in range(CONV_TM // CONV_RC):
            accs = [jnp.zeros((SUBLANES, CONV_CC), F32) for _ in range(CONV_RC // SUBLANES)]
            for j in range(CONV_KERNEL):
                s, lo = (lead + j) % SUBLANES, r * CONV_RC + (lead + j) // SUBLANES * SUBLANES
                w = wb_ref[j, :, cs]
                for q in range(len(accs)):
                    accs[q] = accs[q] + vbuf_ref[s, lo + q * SUBLANES:lo + (q + 1) * SUBLANES, cs] * w
            for q, acc in enumerate(accs):
                conv_ref[r * CONV_RC + q * SUBLANES:r * CONV_RC + (q + 1) * SUBLANES, cs] = acc

    v = conv_ref[...] + bdw_ref[...]
    mu = jnp.mean(v, axis=-1, keepdims=True)
    xc = v - mu
    var = jnp.mean(xc * xc, axis=-1, keepdims=True)
    y = xc * lax.rsqrt(var + EPS) * lng_ref[...] + lnb_ref[...]
    o_ref[...] = (y * jax.nn.sigmoid(y)).astype(BF16)


def _conv(v, w_dw, b_dw, ln_g, ln_b, c, w_ada, b_ada):
    tm = CONV_TM
    halo_blocks = tm // CONV_HALO
    ada_tn = (N_MOD * D_MODEL - ADA_EARLY_COLS) // (SEQ // tm)
    ada_first = ADA_EARLY_COLS // ada_tn
    assert ada_tn % 128 == 0 and ADA_EARLY_COLS % ada_tn == 0
    vec = lambda n: pl.BlockSpec((1, n), lambda i: (0, 0))
    vmem = (2 * tm * CONV_WIDTH * 4 + 2 * CONV_HALO * CONV_WIDTH * 4
            + (SUBLANES * (tm + CONV_HALO) + tm) * CONV_WIDTH * 4 + 2 * tm * CONV_WIDTH * 2
            + 6 * tm * CONV_WIDTH * 4 + 2 * D_MODEL * ada_tn * 4 + D_MODEL * ada_tn * 2 + (6 << 20))
    return pl.pallas_call(
        _conv_kernel,
        grid=(SEQ // tm,),
        in_specs=[
            pl.BlockSpec((tm, CONV_WIDTH), lambda i: (i, 0)),
            pl.BlockSpec((CONV_HALO, CONV_WIDTH), lambda i: (jnp.maximum(i * halo_blocks - 1, 0), 0)),
            pl.BlockSpec((CONV_KERNEL, CONV_WIDTH), lambda i: (0, 0)),
            vec(CONV_WIDTH), vec(CONV_WIDTH), vec(CONV_WIDTH),
            vec(D_MODEL),
            pl.BlockSpec((D_MODEL, ada_tn), lambda i: (0, ada_first + i)),
            pl.BlockSpec((1, ada_tn), lambda i: (0, ada_first + i)),
        ],
        out_specs=[pl.BlockSpec((tm, CONV_WIDTH), lambda i: (i, 0)),
                   pl.BlockSpec((1, ada_tn), lambda i: (0, i))],
        out_shape=[jax.ShapeDtypeStruct((SEQ, CONV_WIDTH), BF16),
                   jax.ShapeDtypeStruct((1, N_MOD * D_MODEL - ADA_EARLY_COLS), F32)],
        scratch_shapes=[pltpu.VMEM((SUBLANES, tm + CONV_HALO, CONV_WIDTH), F32),
                        pltpu.VMEM((tm, CONV_WIDTH), F32),
                        pltpu.VMEM((CONV_KERNEL, SUBLANES, CONV_WIDTH), F32)],
        compiler_params=_params(("parallel",), vmem),
        name="convbranch",
    )(v, v, w_dw, b_dw.reshape(1, -1), ln_g.reshape(1, -1), ln_b.reshape(1, -1),
      c, w_ada, b_ada.reshape(1, -1))


N_SLOPE_PARTS = 3
AUG_LANES = 128
POS_SPLIT = 256
assert POS_SPLIT <= 256 and ATT_TQ // POS_SPLIT <= 256
assert ATT_TQ % ATT_TK == 0 and ATT_TK % ATT_CB == 0 and ATT_CB % CHUNK == 0


def _aug_operand(n_rows, parts, key_side):
    lane = lax.broadcasted_iota(jnp.int32, (n_rows, AUG_LANES), 1)
    r = lax.broadcasted_iota(jnp.int32, (n_rows, AUG_LANES), 0)
    hi = ((r // POS_SPLIT) * POS_SPLIT).astype(F32)
    lo = (r % POS_SPLIT).astype(F32)
    piece = lane % N_SLOPE_PARTS
    part = jnp.where(piece == 0, parts[0], jnp.where(piece == 1, parts[1], parts[2]))
    zero = jnp.zeros((n_rows, AUG_LANES), F32)
    if key_side:
        vals = (-part, -part, hi, lo)
    else:
        vals = (hi, lo, part, part)
    out = zero
    for g, v in enumerate(vals):
        out = jnp.where(lane // N_SLOPE_PARTS == g, v, out)
    return out


def _diagonal_blocks():
    tq, tk, cb = ATT_TQ, ATT_TK, ATT_CB
    nqc = tq // cb
    out = []
    for d in range(tq // tk):
        for b in range(2 * nqc):
            q_lo = (b % nqc) * cb
            if q_lo + cb <= d * tk:
                continue
            if q_lo >= (d + 1) * tk:
                out.append((d, b, None, tk))
            else:
                out.append((d, b, 1 + (q_lo - d * tk) // cb, min(tk, q_lo + cb - d * tk)))
    return out


def _attn_kernel(parts_ref, lam_ref, q_ref, k_ref, vt_ref, g_ref, o_ref,
                 qs_ref, kaug_ref, corr_ref, s_ref, m_ref, l_ref, acc_ref):
    tq, tk, cb = ATT_TQ, ATT_TK, ATT_CB
    nqc = tq // cb
    nb = 2 * nqc
    h = pl.program_id(0)
    i = pl.program_id(1)
    n_before = i * (tq // tk)
    parts = [parts_ref[h * N_SLOPE_PARTS + n] for n in range(N_SLOPE_PARTS)]
    sl = parts[0] + parts[1] + parts[2]

    @pl.when(i == 0)
    def _():
        kaug_ref[...] = _aug_operand(tk, parts, key_side=True).astype(BF16)
        qaug = _aug_operand(tq, parts, key_side=False).astype(BF16)
        qs_ref[0:tq, V_HEAD_DIM:] = qaug
        qs_ref[tq:, V_HEAD_DIM:] = qaug
        corr_ref[0] = jnp.zeros((tk, cb), F32)
        for c in range(tk // cb):
            krel = lax.broadcasted_iota(jnp.int32, (tk, cb), 0)
            qrel = lax.broadcasted_iota(jnp.int32, (tk, cb), 1) + c * cb
            after = jnp.maximum(krel - qrel, 0).astype(F32)
            allowed = (krel // CHUNK) <= (qrel // CHUNK)
            corr_ref[1 + c] = jnp.where(allowed, -2.0 * sl * after, NEG_INF)

    q = q_ref[...]
    lane = lax.broadcasted_iota(jnp.int32, (tq, V_HEAD_DIM), 1)
    zero = jnp.zeros_like(q)
    qs_ref[0:tq, 0:V_HEAD_DIM] = jnp.where(lane < QK_HEAD_DIM, q, zero)
    qs_ref[tq:, 0:V_HEAD_DIM] = jnp.where(lane >= QK_HEAD_DIM, q, zero)

    m_ref[...] = jnp.full_like(m_ref, NEG_INF)
    l_ref[...] = jnp.zeros_like(l_ref)
    acc_ref[...] = jnp.zeros_like(acc_ref)

    def scores(t, b, slot, n_keys=tk):
        k = k_ref[pl.ds(pl.multiple_of(t * tk, tk), n_keys), :]
        kx = jnp.concatenate([k, kaug_ref[0:n_keys, :]], axis=1)
        s_ref[slot, 0:n_keys] = lax.dot_general(kx, qs_ref[b * cb:(b + 1) * cb, :],
                                                (((1,), (1,)), ((), ())),
                                                preferred_element_type=F32)

    def value_tile(t):
        return vt_ref[:, pl.ds(pl.multiple_of(t * tk, tk), tk)]

    def softmax_values(t, b, slot, corr, vt, n_keys=tk):
        off = sl * (i * tq - t * tk).astype(F32)
        s = s_ref[slot, 0:n_keys]
        if corr is not None:
            s = s + corr_ref[corr, 0:n_keys]
        m_prev = m_ref[b]
        m_new = jnp.maximum(m_prev, jnp.max(s, axis=0, keepdims=True) - off)
        alpha = jnp.exp2(m_prev - m_new)
        p = jnp.exp2(s - (m_new + off))
        l_ref[b] = alpha * l_ref[b] + jnp.sum(p, axis=0, keepdims=True)
        m_ref[b] = m_new
        pv = jnp.dot(vt[:, 0:n_keys], p.astype(BF16), preferred_element_type=F32)
        acc_ref[b] = alpha * acc_ref[b] + pv

    blocks = _diagonal_blocks()
    assert [blk[:2] for blk in blocks[:ATT_SKEW_S]] == [(0, b) for b in range(ATT_SKEW_S)]
    assert nb % ATT_RING == 0

    for b in range(ATT_SKEW_S):
        scores(0, b, b % ATT_RING)

    tiles_per_iter = tq // tk

    def body(jj, carry):
        for u in range(tiles_per_iter):
            j = jj * tiles_per_iter + u
            vt = value_tile(j)
            for b in range(nb):
                bs = b + ATT_SKEW_S
                scores(j + bs // nb, bs % nb, bs % ATT_RING)
                softmax_values(j, b, b % ATT_RING, None, vt)
        return carry

    lax.fori_loop(0, i, body, 0)

    vts = {}
    for n, (d, b, corr, n_keys) in enumerate(blocks):
        if d not in vts:
            vts[d] = value_tile(n_before + d)
        if n + ATT_SKEW_S < len(blocks):
            d2, b2, _, n_keys2 = blocks[n + ATT_SKEW_S]
            scores(n_before + d2, b2, (n + ATT_SKEW_S) % ATT_RING, n_keys2)
        softmax_values(n_before + d, b, n % ATT_RING, corr, vts[d], n_keys)

    lam = lam_ref[0]
    for c in range(nqc):
        d = acc_ref[c] / l_ref[c] - lam * (acc_ref[nqc + c] / l_ref[nqc + c])
        ms = jnp.mean(d * d, axis=0, keepdims=True)
        y = d * lax.rsqrt(ms + EPS) * g_ref[...] * (1.0 - LAMBDA_INIT)
        o_ref[c * cb:(c + 1) * cb, :] = y.T.astype(BF16)


def _slope_parts():
    rest = (2.0 ** (-8.0 * (np.arange(N_HEADS, dtype=np.float64) + 1.0) / N_HEADS) * LOG2E).astype(np.float32)
    pieces = []
    for _ in range(N_SLOPE_PARTS):
        piece = rest.astype(jnp.bfloat16).astype(np.float32)
        pieces.append(piece)
        rest = rest - piece
    return np.stack(pieces, axis=1).reshape(-1)


def _attention(qk, vt, slope_parts, lam, subln_g):
    tq, tk, cb = ATT_TQ, ATT_TK, ATT_CB
    nh = N_HEADS
    nb = 2 * tq // cb
    smem = pl.BlockSpec(memory_space=pltpu.SMEM)
    scratch = [
        pltpu.VMEM((2 * tq, V_HEAD_DIM + AUG_LANES), BF16),
        pltpu.VMEM((tk, AUG_LANES), BF16),
        pltpu.VMEM((1 + tk // cb, tk, cb), F32),
        pltpu.VMEM((ATT_RING, tk, cb), F32),
        pltpu.VMEM((nb, 1, cb), F32),
        pltpu.VMEM((nb, 1, cb), F32),
        pltpu.VMEM((nb, V_HEAD_DIM, cb), F32),
    ]
    vmem = (2 * 2 * SEQ * V_HEAD_DIM * 2 + 4 * tq * V_HEAD_DIM * 2
            + 2 * tq * (V_HEAD_DIM + AUG_LANES) * 2 + tk * AUG_LANES * 2 + (1 + tk // cb) * tk * cb * 4
            + ATT_RING * tk * cb * 4 + 2 * nb * 8 * cb * 4
            + nb * V_HEAD_DIM * cb * 4 + 8 * tk * cb * 4 + (6 << 20))
    return pl.pallas_call(
        _attn_kernel,
        grid=(nh, SEQ // tq),
        in_specs=[
            smem, smem,
            pl.BlockSpec((tq, V_HEAD_DIM), lambda h, i: (i, h)),
            pl.BlockSpec((SEQ, V_HEAD_DIM), lambda h, i: (0, nh + h)),
            pl.BlockSpec((V_HEAD_DIM, SEQ), lambda h, i: (h, 0)),
            pl.BlockSpec((V_HEAD_DIM, 1), lambda h, i: (0, 0)),
        ],
        out_specs=pl.BlockSpec((tq, V_HEAD_DIM), lambda h, i: (i, h)),
        out_shape=jax.ShapeDtypeStruct((SEQ, ATTN_WIDTH), BF16),
        scratch_shapes=scratch,
        compiler_params=_params(("parallel", "arbitrary"), vmem),
        name="diffattn",
    )(slope_parts, lam, qk, qk, vt, subln_g.reshape(V_HEAD_DIM, 1))


def _outproj_kernel(x_ref, gate_ref, gpost_ref, yc_ref, ya_ref, wc_ref, wa_ref, o_ref, vec_ref):
    y = jnp.dot(yc_ref[...], wc_ref[...], preferred_element_type=F32)
    o_ref[...] = y + jnp.dot(ya_ref[...], wa_ref[...], preferred_element_type=F32)
    _gated_residual(x_ref, o_ref, gpost_ref, gate_ref, 1.0, o_ref, vec_ref)


def _outproj(x, mod, gate_row, g_post, y_conv, y_attn, w_out):
    tm = OUTPROJ_TM
    vmem = (2 * 2 * tm * D_MODEL * 4 + 2 * 2 * tm * CONV_WIDTH * 2 + 2 * D_MODEL * D_MODEL * 2
            + 3 * tm * D_MODEL * 4 + (6 << 20))
    return pl.pallas_call(
        _outproj_kernel,
        grid=(SEQ // tm,),
        in_specs=[
            pl.BlockSpec((tm, D_MODEL), lambda i: (i, 0)),
            pl.BlockSpec((1, D_MODEL), lambda i: (0, gate_row)),
            pl.BlockSpec((1, D_MODEL), lambda i: (0, 0)),
            pl.BlockSpec((tm, CONV_WIDTH), lambda i: (i, 0)),
            pl.BlockSpec((tm, ATTN_WIDTH), lambda i: (i, 0)),
            pl.BlockSpec((CONV_WIDTH, D_MODEL), lambda i: (0, 0)),
            pl.BlockSpec((ATTN_WIDTH, D_MODEL), lambda i: (1, 0)),
        ],
        out_specs=pl.BlockSpec((tm, D_MODEL), lambda i: (i, 0)),
        out_shape=jax.ShapeDtypeStruct((SEQ, D_MODEL), F32),
        scratch_shapes=[pltpu.VMEM((1, D_MODEL), F32)],
        compiler_params=_params(("parallel",), vmem),
        name="outproj",
    )(x, mod, g_post.reshape(1, D_MODEL), y_conv, y_attn, w_out, w_out)


def kernel(x, c, w_ada, b_ada, g_pre, g_post, w_ffn1_gate, w_ffn1_up, w_ffn1_down, w_in, b_in_conv, w_dw, b_dw, conv_ln_g, conv_ln_b, lam_q1, lam_k1, lam_q2, lam_k2, subln_g, w_out, w_ffn2_gate, w_ffn2_up, w_ffn2_down):
    bf = lambda w: w[0].astype(BF16)
    x2 = x.reshape(SEQ, D_MODEL)

    mod = _ada(c, w_ada[0], b_ada[0], ADA_EARLY_COLS)

    x2 = _ffn(x2, mod, 0, g_pre[0, 0], g_post[0, 0], w_ffn1_gate[0], w_ffn1_up[0], w_ffn1_down[0])

    v_conv, qk, vt = _inproj(x2, mod, g_pre[0, 1], w_in[0], b_in_conv[0])
    y_conv, mod_late = _conv(v_conv, w_dw[0], b_dw[0], conv_ln_g[0], conv_ln_b[0], c, w_ada[0], b_ada[0])

    lam = (jnp.exp(jnp.sum(lam_q1[0].astype(F32) * lam_k1[0].astype(F32)))
           - jnp.exp(jnp.sum(lam_q2[0].astype(F32) * lam_k2[0].astype(F32)))
           + LAMBDA_INIT).reshape(1)
    slope_parts = jnp.asarray(_slope_parts())
    y_attn = _attention(qk, vt, slope_parts, lam, subln_g[0])

    x2 = _outproj(x2, mod_late, 5 - N_MOD_EARLY, g_post[0, 1], y_conv, y_attn, bf(w_out))

    x2 = _ffn(x2, mod_late, 6 - N_MOD_EARLY, g_pre[0, 2], g_post[0, 2],
              w_ffn2_gate[0], w_ffn2_up[0], w_ffn2_down[0])
    return x2.reshape(1, SEQ, D_MODEL)
```

```python
import math

import jax
import jax.numpy as jnp
import numpy as np
from jax import lax
from jax.experimental import pallas as pl
from jax.experimental.pallas import tpu as pltpu

D_MODEL = 2048
SEQ = 8192
CHUNK = 64
CONV_WIDTH = D_MODEL // 2
ATTN_WIDTH = D_MODEL - CONV_WIDTH
CONV_KERNEL = 31
N_HEADS = 8
V_HEAD_DIM = ATTN_WIDTH // N_HEADS
QK_HEAD_DIM = V_HEAD_DIM // 2
D_FF = 5632
N_MOD = 9
IN_WIDTH = 2 * CONV_WIDTH + 3 * ATTN_WIDTH
EPS = 1e-6
NEG_INF = -1e30
LAMBDA_INIT = 0.8 - 0.6 * math.exp(-0.3 * 0)

F32 = jnp.float32
BF16 = jnp.bfloat16

V7X_VMEM_BUDGET_BYTES = 62 * 1024 * 1024

ADA_TN = 1024
N_MOD_EARLY = 5
ADA_EARLY_COLS = N_MOD_EARLY * D_MODEL
FFN_TM = 1024
FFN_TF = 256
INPROJ_TM = 1024
OUTPROJ_TM = 512
PROJ_TN = 1024
CONV_TM = 512
CONV_HALO = 32
CONV_RC = 32
CONV_CC = 512
SUBLANES = 8
ATT_TQ = 2048
ATT_TK = 512
ATT_CB = 256
ATT_SKEW_S = 3
ATT_RING = 4


def _params(semantics, vmem_bytes):
    return pltpu.CompilerParams(
        dimension_semantics=semantics,
        vmem_limit_bytes=min(int(vmem_bytes), V7X_VMEM_BUDGET_BYTES),
    )


ROW_CHUNK = 16


def _norm_rows(src_ref, scale_ref, dst_ref, finish):
    for c in range(src_ref.shape[0] // ROW_CHUNK):
        rows = slice(c * ROW_CHUNK, (c + 1) * ROW_CHUNK)
        x = src_ref[rows, :]
        ms = jnp.mean(x * x, axis=-1, keepdims=True)
        dst_ref[rows, :] = finish(rows, x * lax.rsqrt(ms + EPS) * scale_ref[...]).astype(dst_ref.dtype)


def _modulated_norm(x_ref, gpre_ref, scale_ref, shift_ref, h_ref, vec_ref):
    vec_ref[...] = gpre_ref[...] * (1.0 + scale_ref[...])
    _norm_rows(x_ref, vec_ref, h_ref, lambda rows, y: y + shift_ref[...])


def _gated_residual(x_ref, y_ref, gpost_ref, gate_ref, gate_factor, o_ref, vec_ref):
    vec_ref[...] = gpost_ref[...] * (gate_factor * gate_ref[...])
    _norm_rows(y_ref, vec_ref, o_ref, lambda rows, r: x_ref[rows, :] + r)


def _ada_columns(c_ref, w_ref, b_ref, o_ref):
    c = c_ref[...]
    s = (c * jax.nn.sigmoid(c)).astype(BF16)
    s8 = jnp.broadcast_to(s, (8, D_MODEL))
    r = jnp.dot(s8, w_ref[...].astype(BF16), preferred_element_type=F32)
    o_ref[...] = r[0:1, :] + b_ref[...]


def _ada_kernel(c_ref, w_ref, b_ref, o_ref):
    _ada_columns(c_ref, w_ref, b_ref, o_ref)


def _ada(c, w, b, n):
    return pl.pallas_call(
        _ada_kernel,
        grid=(n // ADA_TN,),
        in_specs=[
            pl.BlockSpec((1, D_MODEL), lambda j: (0, 0)),
            pl.BlockSpec((D_MODEL, ADA_TN), lambda j: (0, j)),
            pl.BlockSpec((1, ADA_TN), lambda j: (0, j)),
        ],
        out_specs=pl.BlockSpec((1, ADA_TN), lambda j: (0, j)),
        out_shape=jax.ShapeDtypeStruct((1, n), F32),
        compiler_params=_params(("parallel",), 2 * D_MODEL * ADA_TN * 4 + D_MODEL * ADA_TN * 2 + (8 << 20)),
        name="adaln",
    )(c, w, b.reshape(1, -1))


def _ffn_kernel(x_ref, shift_ref, scale_ref, gate_ref, gpre_ref, gpost_ref,
                wg_ref, wu_ref, wd_ref, o_ref, h_ref, vec_ref):
    j = pl.program_id(1)
    last = pl.num_programs(1) - 1

    def down_projection():
        h = h_ref[...]
        g = jnp.dot(h, wg_ref[...].astype(BF16), preferred_element_type=F32)
        u = jnp.dot(h, wu_ref[...].astype(BF16), preferred_element_type=F32)
        a = (g * jax.nn.sigmoid(g) * u).astype(BF16)
        return jnp.dot(a, wd_ref[...].astype(BF16), preferred_element_type=F32)

    @pl.when(j == 0)
    def _():
        _modulated_norm(x_ref, gpre_ref, scale_ref, shift_ref, h_ref, vec_ref)
        o_ref[...] = down_projection()

    @pl.when(jnp.logical_and(j > 0, j < last))
    def _():
        o_ref[...] += down_projection()

    @pl.when(j == last)
    def _():
        o_ref[...] += down_projection()
        _gated_residual(x_ref, o_ref, gpost_ref, gate_ref, 0.5, o_ref, vec_ref)


def _ffn(x, mod, mod_base, g_pre, g_post, wg, wu, wd):
    tm, tf = FFN_TM, FFN_TF
    row = lambda k: pl.BlockSpec((1, D_MODEL), lambda i, j, k=k: (0, k))
    vec = pl.BlockSpec((1, D_MODEL), lambda i, j: (0, 0))
    vmem = (2 * 2 * tm * D_MODEL * 4
            + tm * D_MODEL * 2
            + 2 * 3 * D_MODEL * tf * 4
            + 3 * D_MODEL * tf * 2
            + 3 * tm * tf * 4
            + (4 << 20))
    return pl.pallas_call(
        _ffn_kernel,
        grid=(SEQ // tm, D_FF // tf),
        in_specs=[
            pl.BlockSpec((tm, D_MODEL), lambda i, j: (i, 0)),
            row(mod_base), row(mod_base + 1), row(mod_base + 2),
            vec, vec,
            pl.BlockSpec((D_MODEL, tf), lambda i, j: (0, j)),
            pl.BlockSpec((D_MODEL, tf), lambda i, j: (0, j)),
            pl.BlockSpec((tf, D_MODEL), lambda i, j: (j, 0)),
        ],
        out_specs=pl.BlockSpec((tm, D_MODEL), lambda i, j: (i, 0)),
        out_shape=jax.ShapeDtypeStruct((SEQ, D_MODEL), F32),
        scratch_shapes=[pltpu.VMEM((tm, D_MODEL), BF16), pltpu.VMEM((1, D_MODEL), F32)],
        compiler_params=_params(("parallel", "arbitrary"), vmem),
        name="ffn",
    )(x, mod, mod, mod, g_pre.reshape(1, D_MODEL), g_post.reshape(1, D_MODEL), wg, wu, wd)


N_U_BLOCKS = 2 * CONV_WIDTH // PROJ_TN
assert N_U_BLOCKS == 2
N_QK_BLOCKS = 2 * ATTN_WIDTH // PROJ_TN
assert PROJ_TN == ATTN_WIDTH
LOG2E = math.log2(math.e)
Q_SCALE_LOG2 = QK_HEAD_DIM ** -0.5 * LOG2E


def _inproj_kernel(x_ref, shift_ref, scale_ref, gpre_ref, w_ref, bin_ref, glu_ref, qk_ref, vt_ref,
                   h_ref, vec_ref, ua_ref):
    j = pl.program_id(1)

    @pl.when(j == 0)
    def _():
        _modulated_norm(x_ref, gpre_ref, scale_ref, shift_ref, h_ref, vec_ref)
        ua_ref[...] = jnp.dot(h_ref[...], w_ref[...].astype(BF16), preferred_element_type=F32)

    @pl.when(j == 1)
    def _():
        ug = jnp.dot(h_ref[...], w_ref[...].astype(BF16), preferred_element_type=F32)
        a = ua_ref[...] + bin_ref[:, 0:CONV_WIDTH]
        glu_ref[...] = a * jax.nn.sigmoid(ug + bin_ref[:, CONV_WIDTH:])

    @pl.when(jnp.logical_and(j >= N_U_BLOCKS, j < N_U_BLOCKS + N_QK_BLOCKS))
    def _():
        c = jnp.where(j == N_U_BLOCKS, Q_SCALE_LOG2, 1.0)
        qk_ref[...] = (jnp.dot(h_ref[...], w_ref[...].astype(BF16), preferred_element_type=F32) * c).astype(BF16)

    @pl.when(j == N_U_BLOCKS + N_QK_BLOCKS)
    def _():
        v = jnp.dot(h_ref[...], w_ref[...].astype(BF16), preferred_element_type=F32)
        vt_ref[...] = v.T.astype(BF16)


def _inproj(x, mod, g_pre, w_in, b_in):
    tm, tn = INPROJ_TM, PROJ_TN
    n_main = N_U_BLOCKS + N_QK_BLOCKS
    vmem = (2 * tm * D_MODEL * 4 + tm * D_MODEL * 2 + 2 * D_MODEL * tn * 4 + D_MODEL * tn * 2
            + 2 * tm * tn * (4 + 2) + 2 * ATTN_WIDTH * tm * 2 + 3 * tm * tn * 4 + (4 << 20))
    return pl.pallas_call(
        _inproj_kernel,
        grid=(SEQ // tm, n_main + 1),
        in_specs=[
            pl.BlockSpec((tm, D_MODEL), lambda i, j: (i, 0)),
            pl.BlockSpec((1, D_MODEL), lambda i, j: (0, 3)),
            pl.BlockSpec((1, D_MODEL), lambda i, j: (0, 4)),
            pl.BlockSpec((1, D_MODEL), lambda i, j: (0, 0)),
            pl.BlockSpec((D_MODEL, tn), lambda i, j: (0, j)),
            pl.BlockSpec((1, 2 * CONV_WIDTH), lambda i, j: (0, 0)),
        ],
        out_specs=[
            pl.BlockSpec((tm, CONV_WIDTH), lambda i, j: (i, 0)),
            pl.BlockSpec((tm, tn), lambda i, j: (i, jnp.clip(j - N_U_BLOCKS, 0, N_QK_BLOCKS - 1))),
            pl.BlockSpec((ATTN_WIDTH, tm), lambda i, j: (0, i)),
        ],
        out_shape=[
            jax.ShapeDtypeStruct((SEQ, CONV_WIDTH), F32),
            jax.ShapeDtypeStruct((SEQ, 2 * ATTN_WIDTH), BF16),
            jax.ShapeDtypeStruct((ATTN_WIDTH, SEQ), BF16),
        ],
        scratch_shapes=[pltpu.VMEM((tm, D_MODEL), BF16), pltpu.VMEM((1, D_MODEL), F32),
                        pltpu.VMEM((tm, CONV_WIDTH), F32)],
        compiler_params=_params(("parallel", "arbitrary"), vmem),
        name="inproj",
    )(x, mod, mod, g_pre.reshape(1, D_MODEL), w_in, b_in.reshape(1, -1))


def _conv_kernel(v_ref, vprev_ref, wdw_ref, bdw_ref, lng_ref, lnb_ref, c_ref, wada_ref, bada_ref,
                 o_ref, mod_ref, vbuf_ref, conv_ref, wb_ref):
    i = pl.program_id(0)
    _ada_columns(c_ref, wada_ref, bada_ref, mod_ref)
    vbuf_ref[0, 0:CONV_HALO, :] = jnp.where(i > 0, vprev_ref[...], 0.0)
    vbuf_ref[0, CONV_HALO:, :] = v_ref[...]
    n_shift = CONV_TM + CONV_HALO - SUBLANES
    for s in range(1, SUBLANES):
        vbuf_ref[s, 0:n_shift, :] = vbuf_ref[0, s:s + n_shift, :]

    for j in range(CONV_KERNEL):
        wb_ref[j] = jnp.broadcast_to(wdw_ref[j:j + 1, :], (SUBLANES, CONV_WIDTH))

    lead = CONV_HALO - (CONV_KERNEL - 1)
    for cb in range(CONV_WIDTH // CONV_CC):
        cs = slice(cb * CONV_CC, (cb + 1) * CONV_CC)
        for r in range(CONV_TM // CONV_RC):
            accs = [jnp.zeros((SUBLANES, CONV_CC), F32) for _ in range(CONV_RC // SUBLANES)]
            for j in range(CONV_KERNEL):
                s, lo = (lead + j) % SUBLANES, r * CONV_RC + (lead + j) // SUBLANES * SUBLANES
                w = wb_ref[j, :, cs]
                for q in range(len(accs)):
                    accs[q] = accs[q] + vbuf_ref[s, lo + q * SUBLANES:lo + (q + 1) * SUBLANES, cs] * w
            for q, acc in enumerate(accs):
                conv_ref[r * CONV_RC + q * SUBLANES:r * CONV_RC + (q + 1) * SUBLANES, cs] = acc

    v = conv_ref[...] + bdw_ref[...]
    mu = jnp.mean(v, axis=-1, keepdims=True)
    xc = v - mu
    var = jnp.mean(xc * xc, axis=-1, keepdims=True)
    y = xc * lax.rsqrt(var + EPS) * lng_ref[...] + lnb_ref[...]
    o_ref[...] = (y * jax.nn.sigmoid(y)).astype(BF16)


def _conv(v, w_dw, b_dw, ln_g, ln_b, c, w_ada, b_ada):
    tm = CONV_TM
    halo_blocks = tm // CONV_HALO
    ada_tn = (N_MOD * D_MODEL - ADA_EARLY_COLS) // (SEQ // tm)
    ada_first = ADA_EARLY_COLS // ada_tn
    assert ada_tn % 128 == 0 and ADA_EARLY_COLS % ada_tn == 0
    vec = lambda n: pl.BlockSpec((1, n), lambda i: (0, 0))
    vmem = (2 * tm * CONV_WIDTH * 4 + 2 * CONV_HALO * CONV_WIDTH * 4
            + (SUBLANES * (tm + CONV_HALO) + tm) * CONV_WIDTH * 4 + 2 * tm * CONV_WIDTH * 2
            + 6 * tm * CONV_WIDTH * 4 + 2 * D_MODEL * ada_tn * 4 + D_MODEL * ada_tn * 2 + (6 << 20))
    return pl.pallas_call(
        _conv_kernel,
        grid=(SEQ // tm,),
        in_specs=[
            pl.BlockSpec((tm, CONV_WIDTH), lambda i: (i, 0)),
            pl.BlockSpec((CONV_HALO, CONV_WIDTH), lambda i: (jnp.maximum(i * halo_blocks - 1, 0), 0)),
            pl.BlockSpec((CONV_KERNEL, CONV_WIDTH), lambda i: (0, 0)),
            vec(CONV_WIDTH), vec(CONV_WIDTH), vec(CONV_WIDTH),
            vec(D_MODEL),
            pl.BlockSpec((D_MODEL, ada_tn), lambda i: (0, ada_first + i)),
            pl.BlockSpec((1, ada_tn), lambda i: (0, ada_first + i)),
        ],
        out_specs=[pl.BlockSpec((tm, CONV_WIDTH), lambda i: (i, 0)),
                   pl.BlockSpec((1, ada_tn), lambda i: (0, i))],
        out_shape=[jax.ShapeDtypeStruct((SEQ, CONV_WIDTH), BF16),
                   jax.ShapeDtypeStruct((1, N_MOD * D_MODEL - ADA_EARLY_COLS), F32)],
        scratch_shapes=[pltpu.VMEM((SUBLANES, tm + CONV_HALO, CONV_WIDTH), F32),
                        pltpu.VMEM((tm, CONV_WIDTH), F32),
                        pltpu.VMEM((CONV_KERNEL, SUBLANES, CONV_WIDTH), F32)],
        compiler_params=_params(("parallel",), vmem),
        name="convbranch",
    )(v, v, w_dw, b_dw.reshape(1, -1), ln_g.reshape(1, -1), ln_b.reshape(1, -1),
      c, w_ada, b_ada.reshape(1, -1))


N_SLOPE_PARTS = 3
AUG_LANES = 128
POS_SPLIT = 256
assert POS_SPLIT <= 256 and ATT_TQ // POS_SPLIT <= 256
assert ATT_TQ % ATT_TK == 0 and ATT_TK % ATT_CB == 0 and ATT_CB % CHUNK == 0


def _aug_operand(n_rows, parts, key_side):
    lane = lax.broadcasted_iota(jnp.int32, (n_rows, AUG_LANES), 1)
    r = lax.broadcasted_iota(jnp.int32, (n_rows, AUG_LANES), 0)
    hi = ((r // POS_SPLIT) * POS_SPLIT).astype(F32)
    lo = (r % POS_SPLIT).astype(F32)
    piece = lane % N_SLOPE_PARTS
    part = jnp.where(piece == 0, parts[0], jnp.where(piece == 1, parts[1], parts[2]))
    zero = jnp.zeros((n_rows, AUG_LANES), F32)
    if key_side:
        vals = (-part, -part, hi, lo)
    else:
        vals = (hi, lo, part, part)
    out = zero
    for g, v in enumerate(vals):
        out = jnp.where(lane // N_SLOPE_PARTS == g, v, out)
    return out


def _diagonal_blocks():
    tq, tk, cb = ATT_TQ, ATT_TK, ATT_CB
    nqc = tq // cb
    out = []
    for d in range(tq // tk):
        for b in range(2 * nqc):
            q_lo = (b % nqc) * cb
            if q_lo + cb <= d * tk:
                continue
            if q_lo >= (d + 1) * tk:
                out.append((d, b, None, tk))
            else:
                out.append((d, b, 1 + (q_lo - d * tk) // cb, min(tk, q_lo + cb - d * tk)))
    return out


def _attn_kernel(parts_ref, lam_ref, q_ref, k_ref, vt_ref, g_ref, o_ref,
                 qs_ref, kaug_ref, corr_ref, s_ref, m_ref, l_ref, acc_ref):
    tq, tk, cb = ATT_TQ, ATT_TK, ATT_CB
    nqc = tq // cb
    nb = 2 * nqc
    h = pl.program_id(0)
    i = pl.program_id(1)
    n_before = i * (tq // tk)
    parts = [parts_ref[h * N_SLOPE_PARTS + n] for n in range(N_SLOPE_PARTS)]
    sl = parts[0] + parts[1] + parts[2]

    @pl.when(i == 0)
    def _():
        kaug_ref[...] = _aug_operand(tk, parts, key_side=True).astype(BF16)
        qaug = _aug_operand(tq, parts, key_side=False).astype(BF16)
        qs_ref[0:tq, V_HEAD_DIM:] = qaug
        qs_ref[tq:, V_HEAD_DIM:] = qaug
        corr_ref[0] = jnp.zeros((tk, cb), F32)
        for c in range(tk // cb):
            krel = lax.broadcasted_iota(jnp.int32, (tk, cb), 0)
            qrel = lax.broadcasted_iota(jnp.int32, (tk, cb), 1) + c * cb
            after = jnp.maximum(krel - qrel, 0).astype(F32)
            allowed = (krel // CHUNK) <= (qrel // CHUNK)
            corr_ref[1 + c] = jnp.where(allowed, -2.0 * sl * after, NEG_INF)

    q = q_ref[...]
    lane = lax.broadcasted_iota(jnp.int32, (tq, V_HEAD_DIM), 1)
    zero = jnp.zeros_like(q)
    qs_ref[0:tq, 0:V_HEAD_DIM] = jnp.where(lane < QK_HEAD_DIM, q, zero)
    qs_ref[tq:, 0:V_HEAD_DIM] = jnp.where(lane >= QK_HEAD_DIM, q, zero)

    m_ref[...] = jnp.full_like(m_ref, NEG_INF)
    l_ref[...] = jnp.zeros_like(l_ref)
    acc_ref[...] = jnp.zeros_like(acc_ref)

    def scores(t, b, slot, n_keys=tk):
        k = k_ref[pl.ds(pl.multiple_of(t * tk, tk), n_keys), :]
        kx = jnp.concatenate([k, kaug_ref[0:n_keys, :]], axis=1)
        s_ref[slot, 0:n_keys] = lax.dot_general(kx, qs_ref[b * cb:(b + 1) * cb, :],
                                                (((1,), (1,)), ((), ())),
                                                preferred_element_type=F32)

    def value_tile(t):
        return vt_ref[:, pl.ds(pl.multiple_of(t * tk, tk), tk)]

    def softmax_values(t, b, slot, corr, vt, n_keys=tk):
        off = sl * (i * tq - t * tk).astype(F32)
        s = s_ref[slot, 0:n_keys]
        if corr is not None:
            s = s + corr_ref[corr, 0:n_keys]
        m_prev = m_ref[b]
        m_new = jnp.maximum(m_prev, jnp.max(s, axis=0, keepdims=True) - off)
        alpha = jnp.exp2(m_prev - m_new)
        p = jnp.exp2(s - (m_new + off))
        l_ref[b] = alpha * l_ref[b] + jnp.sum(p, axis=0, keepdims=True)
        m_ref[b] = m_new
        pv = jnp.dot(vt[:, 0:n_keys], p.astype(BF16), preferred_element_type=F32)
        acc_ref[b] = alpha * acc_ref[b] + pv

    blocks = _diagonal_blocks()
    assert [blk[:2] for blk in blocks[:ATT_SKEW_S]] == [(0, b) for b in range(ATT_SKEW_S)]
    assert nb % ATT_RING == 0

    for b in range(ATT_SKEW_S):
        scores(0, b, b % ATT_RING)

    tiles_per_iter = tq // tk

    def body(jj, carry):
        for u in range(tiles_per_iter):
            j = jj * tiles_per_iter + u
            vt = value_tile(j)
            for b in range(nb):
                bs = b + ATT_SKEW_S
                scores(j + bs // nb, bs % nb, bs % ATT_RING)
                softmax_values(j, b, b % ATT_RING, None, vt)
        return carry

    lax.fori_loop(0, i, body, 0)

    vts = {}
    for n, (d, b, corr, n_keys) in enumerate(blocks):
        if d not in vts:
            vts[d] = value_tile(n_before + d)
        if n + ATT_SKEW_S < len(blocks):
            d2, b2, _, n_keys2 = blocks[n + ATT_SKEW_S]
            scores(n_before + d2, b2, (n + ATT_SKEW_S) % ATT_RING, n_keys2)
        softmax_values(n_before + d, b, n % ATT_RING, corr, vts[d], n_keys)

    lam = lam_ref[0]
    for c in range(nqc):
        d = acc_ref[c] / l_ref[c] - lam * (acc_ref[nqc + c] / l_ref[nqc + c])
        ms = jnp.mean(d * d, axis=0, keepdims=True)
        y = d * lax.rsqrt(ms + EPS) * g_ref[...] * (1.0 - LAMBDA_INIT)
        o_ref[c * cb:(c + 1) * cb, :] = y.T.astype(BF16)


def _slope_parts():
    rest = (2.0 ** (-8.0 * (np.arange(N_HEADS, dtype=np.float64) + 1.0) / N_HEADS) * LOG2E).astype(np.float32)
    pieces = []
    for _ in range(N_SLOPE_PARTS):
        piece = rest.astype(jnp.bfloat16).astype(np.float32)
        pieces.append(piece)
        rest = rest - piece
    return np.stack(pieces, axis=1).reshape(-1)


def _attention(qk, vt, slope_parts, lam, subln_g):
    tq, tk, cb = ATT_TQ, ATT_TK, ATT_CB
    nh = N_HEADS
    nb = 2 * tq // cb
    smem = pl.BlockSpec(memory_space=pltpu.SMEM)
    scratch = [
        pltpu.VMEM((2 * tq, V_HEAD_DIM + AUG_LANES), BF16),
        pltpu.VMEM((tk, AUG_LANES), BF16),
        pltpu.VMEM((1 + tk // cb, tk, cb), F32),
        pltpu.VMEM((ATT_RING, tk, cb), F32),
        pltpu.VMEM((nb, 1, cb), F32),
        pltpu.VMEM((nb, 1, cb), F32),
        pltpu.VMEM((nb, V_HEAD_DIM, cb), F32),
    ]
    vmem = (2 * 2 * SEQ * V_HEAD_DIM * 2 + 4 * tq * V_HEAD_DIM * 2
            + 2 * tq * (V_HEAD_DIM + AUG_LANES) * 2 + tk * AUG_LANES * 2 + (1 + tk // cb) * tk * cb * 4
            + ATT_RING * tk * cb * 4 + 2 * nb * 8 * cb * 4
            + nb * V_HEAD_DIM * cb * 4 + 8 * tk * cb * 4 + (6 << 20))
    return pl.pallas_call(
        _attn_kernel,
        grid=(nh, SEQ // tq),
        in_specs=[
            smem, smem,
            pl.BlockSpec((tq, V_HEAD_DIM), lambda h, i: (i, h)),
            pl.BlockSpec((SEQ, V_HEAD_DIM), lambda h, i: (0, nh + h)),
            pl.BlockSpec((V_HEAD_DIM, SEQ), lambda h, i: (h, 0)),
            pl.BlockSpec((V_HEAD_DIM, 1), lambda h, i: (0, 0)),
        ],
        out_specs=pl.BlockSpec((tq, V_HEAD_DIM), lambda h, i: (i, h)),
        out_shape=jax.ShapeDtypeStruct((SEQ, ATTN_WIDTH), BF16),
        scratch_shapes=scratch,
        compiler_params=_params(("parallel", "arbitrary"), vmem),
        name="diffattn",
    )(slope_parts, lam, qk, qk, vt, subln_g.reshape(V_HEAD_DIM, 1))


def _outproj_kernel(x_ref, gate_ref, gpost_ref, yc_ref, ya_ref, w_ref, o_ref, vec_ref, wbf_ref):
    @pl.when(pl.program_id(0) == 0)
    def _():
        wbf_ref[...] = w_ref[...].astype(BF16)

    y = jnp.dot(yc_ref[...], wbf_ref[0:CONV_WIDTH, :], preferred_element_type=F32)
    o_ref[...] = y + jnp.dot(ya_ref[...], wbf_ref[CONV_WIDTH:, :], preferred_element_type=F32)
    _gated_residual(x_ref, o_ref, gpost_ref, gate_ref, 1.0, o_ref, vec_ref)


def _outproj(x, mod, gate_row, g_post, y_conv, y_attn, w_out):
    tm = OUTPROJ_TM
    vmem = (2 * 2 * tm * D_MODEL * 4 + 2 * 2 * tm * CONV_WIDTH * 2 + D_MODEL * D_MODEL * (4 + 2)
            + 3 * tm * D_MODEL * 4 + (6 << 20))
    return pl.pallas_call(
        _outproj_kernel,
        grid=(SEQ // tm,),
        in_specs=[
            pl.BlockSpec((tm, D_MODEL), lambda i: (i, 0)),
            pl.BlockSpec((1, D_MODEL), lambda i: (0, gate_row)),
            pl.BlockSpec((1, D_MODEL), lambda i: (0, 0)),
            pl.BlockSpec((tm, CONV_WIDTH), lambda i: (i, 0)),
            pl.BlockSpec((tm, ATTN_WIDTH), lambda i: (i, 0)),
            pl.BlockSpec((D_MODEL, D_MODEL), lambda i: (0, 0), pipeline_mode=pl.Buffered(1)),
        ],
        out_specs=pl.BlockSpec((tm, D_MODEL), lambda i: (i, 0)),
        out_shape=jax.ShapeDtypeStruct((SEQ, D_MODEL), F32),
        scratch_shapes=[pltpu.VMEM((1, D_MODEL), F32), pltpu.VMEM((D_MODEL, D_MODEL), BF16)],
        compiler_params=_params(("arbitrary",), vmem),
        name="outproj",
    )(x, mod, g_post.reshape(1, D_MODEL), y_conv, y_attn, w_out)


def kernel(x, c, w_ada, b_ada, g_pre, g_post, w_ffn1_gate, w_ffn1_up, w_ffn1_down, w_in, b_in_conv, w_dw, b_dw, conv_ln_g, conv_ln_b, lam_q1, lam_k1, lam_q2, lam_k2, subln_g, w_out, w_ffn2_gate, w_ffn2_up, w_ffn2_down):
    x2 = x.reshape(SEQ, D_MODEL)

    mod = _ada(c, w_ada[0], b_ada[0], ADA_EARLY_COLS)

    x2 = _ffn(x2, mod, 0, g_pre[0, 0], g_post[0, 0], w_ffn1_gate[0], w_ffn1_up[0], w_ffn1_down[0])

    v_conv, qk, vt = _inproj(x2, mod, g_pre[0, 1], w_in[0], b_in_conv[0])
    y_conv, mod_late = _conv(v_conv, w_dw[0], b_dw[0], conv_ln_g[0], conv_ln_b[0], c, w_ada[0], b_ada[0])

    lam = (jnp.exp(jnp.sum(lam_q1[0].astype(F32) * lam_k1[0].astype(F32)))
           - jnp.exp(jnp.sum(lam_q2[0].astype(F32) * lam_k2[0].astype(F32)))
           + LAMBDA_INIT).reshape(1)
    slope_parts = jnp.asarray(_slope_parts())
    y_attn = _attention(qk, vt, slope_parts, lam, subln_g[0])

    x2 = _outproj(x2, mod_late, 5 - N_MOD_EARLY, g_post[0, 1], y_conv, y_attn, w_out[0])

    x2 = _ffn(x2, mod_late, 6 - N_MOD_EARLY, g_pre[0, 2], g_post[0, 2],
              w_ffn2_gate[0], w_ffn2_up[0], w_ffn2_down[0])
    return x2.reshape(1, SEQ, D_MODEL)
```

```python
import math

import jax
import jax.numpy as jnp
import numpy as np
from jax import lax
from jax.experimental import pallas as pl
from jax.experimental.pallas import tpu as pltpu

D_MODEL = 2048
SEQ = 8192
CHUNK = 64
CONV_WIDTH = D_MODEL // 2
ATTN_WIDTH = D_MODEL - CONV_WIDTH
CONV_KERNEL = 31
N_HEADS = 8
V_HEAD_DIM = ATTN_WIDTH // N_HEADS
QK_HEAD_DIM = V_HEAD_DIM // 2
D_FF = 5632
N_MOD = 9
IN_WIDTH = 2 * CONV_WIDTH + 3 * ATTN_WIDTH
EPS = 1e-6
NEG_INF = -1e30
LAMBDA_INIT = 0.8 - 0.6 * math.exp(-0.3 * 0)

F32 = jnp.float32
BF16 = jnp.bfloat16

V7X_VMEM_BUDGET_BYTES = 62 * 1024 * 1024

ADA_TN = 1024
N_MOD_EARLY = 5
ADA_EARLY_COLS = N_MOD_EARLY * D_MODEL
FFN_TM = 1024
FFN_TF = 256
INPROJ_TM = 1024
OUTPROJ_TM = 512
PROJ_TN = 1024
CONV_TM = 512
CONV_HALO = 32
CONV_RC = 32
CONV_CC = 512
SUBLANES = 8
ATT_TQ = 2048
ATT_TK = 512
ATT_CB = 256
ATT_SKEW_S = 3
ATT_RING = 4


def _params(semantics, vmem_bytes):
    return pltpu.CompilerParams(
        dimension_semantics=semantics,
        vmem_limit_bytes=min(int(vmem_bytes), V7X_VMEM_BUDGET_BYTES),
    )


ROW_CHUNK = 16


def _norm_rows(src_ref, scale_ref, dst_ref, finish):
    for c in range(src_ref.shape[0] // ROW_CHUNK):
        rows = slice(c * ROW_CHUNK, (c + 1) * ROW_CHUNK)
        x = src_ref[rows, :]
        ms = jnp.mean(x * x, axis=-1, keepdims=True)
        dst_ref[rows, :] = finish(rows, x * lax.rsqrt(ms + EPS) * scale_ref[...]).astype(dst_ref.dtype)


def _modulated_norm(x_ref, gpre_ref, scale_ref, shift_ref, h_ref, vec_ref):
    vec_ref[...] = gpre_ref[...] * (1.0 + scale_ref[...])
    _norm_rows(x_ref, vec_ref, h_ref, lambda rows, y: y + shift_ref[...])


def _gated_residual(x_ref, y_ref, gpost_ref, gate_ref, gate_factor, o_ref, vec_ref):
    vec_ref[...] = gpost_ref[...] * (gate_factor * gate_ref[...])
    _norm_rows(y_ref, vec_ref, o_ref, lambda rows, r: x_ref[rows, :] + r)


def _ada_columns(c_ref, w_ref, b_ref, o_ref):
    c = c_ref[...]
    s = (c * jax.nn.sigmoid(c)).astype(BF16)
    s8 = jnp.broadcast_to(s, (8, D_MODEL))
    r = jnp.dot(s8, w_ref[...].astype(BF16), preferred_element_type=F32)
    o_ref[...] = r[0:1, :] + b_ref[...]


def _ada_kernel(c_ref, w_ref, b_ref, o_ref):
    _ada_columns(c_ref, w_ref, b_ref, o_ref)


def _ada(c, w, b, n):
    return pl.pallas_call(
        _ada_kernel,
        grid=(n // ADA_TN,),
        in_specs=[
            pl.BlockSpec((1, D_MODEL), lambda j: (0, 0)),
            pl.BlockSpec((D_MODEL, ADA_TN), lambda j: (0, j)),
            pl.BlockSpec((1, ADA_TN), lambda j: (0, j)),
        ],
        out_specs=pl.BlockSpec((1, ADA_TN), lambda j: (0, j)),
        out_shape=jax.ShapeDtypeStruct((1, n), F32),
        compiler_params=_params(("parallel",), 2 * D_MODEL * ADA_TN * 4 + D_MODEL * ADA_TN * 2 + (8 << 20)),
        name="adaln",
    )(c, w, b.reshape(1, -1))


def _ffn_kernel(x_ref, shift_ref, scale_ref, gate_ref, gpre_ref, gpost_ref,
                wg_ref, wu_ref, wd_ref, o_ref, h_ref, vec_ref):
    j = pl.program_id(1)
    last = pl.num_programs(1) - 1

    def down_projection():
        h = h_ref[...]
        g = jnp.dot(h, wg_ref[...].astype(BF16), preferred_element_type=F32)
        u = jnp.dot(h, wu_ref[...].astype(BF16), preferred_element_type=F32)
        a = (g * jax.nn.sigmoid(g) * u).astype(BF16)
        return jnp.dot(a, wd_ref[...].astype(BF16), preferred_element_type=F32)

    @pl.when(j == 0)
    def _():
        _modulated_norm(x_ref, gpre_ref, scale_ref, shift_ref, h_ref, vec_ref)
        o_ref[...] = down_projection()

    @pl.when(jnp.logical_and(j > 0, j < last))
    def _():
        o_ref[...] += down_projection()

    @pl.when(j == last)
    def _():
        o_ref[...] += down_projection()
        _gated_residual(x_ref, o_ref, gpost_ref, gate_ref, 0.5, o_ref, vec_ref)


def _ffn(x, mod, mod_base, g_pre, g_post, wg, wu, wd):
    tm, tf = FFN_TM, FFN_TF
    row = lambda k: pl.BlockSpec((1, D_MODEL), lambda i, j, k=k: (0, k))
    vec = pl.BlockSpec((1, D_MODEL), lambda i, j: (0, 0))
    vmem = (2 * 2 * tm * D_MODEL * 4
            + tm * D_MODEL * 2
            + 2 * 3 * D_MODEL * tf * 4
            + 3 * D_MODEL * tf * 2
            + 3 * tm * tf * 4
            + (4 << 20))
    return pl.pallas_call(
        _ffn_kernel,
        grid=(SEQ // tm, D_FF // tf),
        in_specs=[
            pl.BlockSpec((tm, D_MODEL), lambda i, j: (i, 0)),
            row(mod_base), row(mod_base + 1), row(mod_base + 2),
            vec, vec,
            pl.BlockSpec((D_MODEL, tf), lambda i, j: (0, j)),
            pl.BlockSpec((D_MODEL, tf), lambda i, j: (0, j)),
            pl.BlockSpec((tf, D_MODEL), lambda i, j: (j, 0)),
        ],
        out_specs=pl.BlockSpec((tm, D_MODEL), lambda i, j: (i, 0)),
        out_shape=jax.ShapeDtypeStruct((SEQ, D_MODEL), F32),
        scratch_shapes=[pltpu.VMEM((tm, D_MODEL), BF16), pltpu.VMEM((1, D_MODEL), F32)],
        compiler_params=_params(("parallel", "arbitrary"), vmem),
        name="ffn",
    )(x, mod, mod, mod, g_pre.reshape(1, D_MODEL), g_post.reshape(1, D_MODEL), wg, wu, wd)


N_U_BLOCKS = 2 * CONV_WIDTH // PROJ_TN
assert N_U_BLOCKS == 2
N_QK_BLOCKS = 2 * ATTN_WIDTH // PROJ_TN
assert PROJ_TN == ATTN_WIDTH
LOG2E = math.log2(math.e)
Q_SCALE_LOG2 = QK_HEAD_DIM ** -0.5 * LOG2E


def _inproj_kernel(x_ref, shift_ref, scale_ref, gpre_ref, w_ref, bin_ref, glu_ref, qk_ref, vt_ref,
                   h_ref, vec_ref, ua_ref):
    j = pl.program_id(1)

    @pl.when(j == 0)
    def _():
        _modulated_norm(x_ref, gpre_ref, scale_ref, shift_ref, h_ref, vec_ref)
        ua_ref[...] = jnp.dot(h_ref[...], w_ref[...].astype(BF16), preferred_element_type=F32)

    @pl.when(j == 1)
    def _():
        ug = jnp.dot(h_ref[...], w_ref[...].astype(BF16), preferred_element_type=F32)
        a = ua_ref[...] + bin_ref[:, 0:CONV_WIDTH]
        glu_ref[...] = a * jax.nn.sigmoid(ug + bin_ref[:, CONV_WIDTH:])

    @pl.when(jnp.logical_and(j >= N_U_BLOCKS, j < N_U_BLOCKS + N_QK_BLOCKS))
    def _():
        c = jnp.where(j == N_U_BLOCKS, Q_SCALE_LOG2, 1.0)
        qk_ref[...] = (jnp.dot(h_ref[...], w_ref[...].astype(BF16), preferred_element_type=F32) * c).astype(BF16)

    @pl.when(j == N_U_BLOCKS + N_QK_BLOCKS)
    def _():
        v = jnp.dot(h_ref[...], w_ref[...].astype(BF16), preferred_element_type=F32)
        vt_ref[...] = v.T.astype(BF16)


def _inproj(x, mod, g_pre, w_in, b_in):
    tm, tn = INPROJ_TM, PROJ_TN
    n_main = N_U_BLOCKS + N_QK_BLOCKS
    vmem = (2 * tm * D_MODEL * 4 + tm * D_MODEL * 2 + 2 * D_MODEL * tn * 4 + D_MODEL * tn * 2
            + 2 * tm * tn * (4 + 2) + 2 * ATTN_WIDTH * tm * 2 + 3 * tm * tn * 4 + (4 << 20))
    return pl.pallas_call(
        _inproj_kernel,
        grid=(SEQ // tm, n_main + 1),
        in_specs=[
            pl.BlockSpec((tm, D_MODEL), lambda i, j: (i, 0)),
            pl.BlockSpec((1, D_MODEL), lambda i, j: (0, 3)),
            pl.BlockSpec((1, D_MODEL), lambda i, j: (0, 4)),
            pl.BlockSpec((1, D_MODEL), lambda i, j: (0, 0)),
            pl.BlockSpec((D_MODEL, tn), lambda i, j: (0, j)),
            pl.BlockSpec((1, 2 * CONV_WIDTH), lambda i, j: (0, 0)),
        ],
        out_specs=[
            pl.BlockSpec((tm, CONV_WIDTH), lambda i, j: (i, 0)),
            pl.BlockSpec((tm, tn), lambda i, j: (i, jnp.clip(j - N_U_BLOCKS, 0, N_QK_BLOCKS - 1))),
            pl.BlockSpec((ATTN_WIDTH, tm), lambda i, j: (0, i)),
        ],
        out_shape=[
            jax.ShapeDtypeStruct((SEQ, CONV_WIDTH), F32),
            jax.ShapeDtypeStruct((SEQ, 2 * ATTN_WIDTH), BF16),
            jax.ShapeDtypeStruct((ATTN_WIDTH, SEQ), BF16),
        ],
        scratch_shapes=[pltpu.VMEM((tm, D_MODEL), BF16), pltpu.VMEM((1, D_MODEL), F32),
                        pltpu.VMEM((tm, CONV_WIDTH), F32)],
        compiler_params=_params(("parallel", "arbitrary"), vmem),
        name="inproj",
    )(x, mod, mod, g_pre.reshape(1, D_MODEL), w_in, b_in.reshape(1, -1))


def _conv_kernel(v_ref, vprev_ref, wdw_ref, bdw_ref, lng_ref, lnb_ref, c_ref, wada_ref, bada_ref,
                 o_ref, mod_ref, vbuf_ref, conv_ref, wb_ref):
    i = pl.program_id(0)
    _ada_columns(c_ref, wada_ref, bada_ref, mod_ref)
    vbuf_ref[0, 0:CONV_HALO, :] = jnp.where(i > 0, vprev_ref[...], 0.0)
    vbuf_ref[0, CONV_HALO:, :] = v_ref[...]
    n_shift = CONV_TM + CONV_HALO - SUBLANES
    for s in range(1, SUBLANES):
        vbuf_ref[s, 0:n_shift, :] = vbuf_ref[0, s:s + n_shift, :]

    for j in range(CONV_KERNEL):
        wb_ref[j] = jnp.broadcast_to(wdw_ref[j:j + 1, :], (SUBLANES, CONV_WIDTH))

    lead = CONV_HALO - (CONV_KERNEL - 1)
    for cb in range(CONV_WIDTH // CONV_CC):
        cs = slice(cb * CONV_CC, (cb + 1) * CONV_CC)
        for r in range(CONV_TM // CONV_RC):
            accs = [jnp.zeros((SUBLANES, CONV_CC), F32) for _ in range(CONV_RC // SUBLANES)]
            for j in range(CONV_KERNEL):
                s, lo = (lead + j) % SUBLANES, r * CONV_RC + (lead + j) // SUBLANES * SUBLANES
                w = wb_ref[j, :, cs]
                for q in range(len(accs)):
                    accs[q] = accs[q] + vbuf_ref[s, lo + q * SUBLANES:lo + (q + 1) * SUBLANES, cs] * w
            for q, acc in enumerate(accs):
                conv_ref[r * CONV_RC + q * SUBLANES:r * CONV_RC + (q + 1) * SUBLANES, cs] = acc

    v = conv_ref[...] + bdw_ref[...]
    mu = jnp.mean(v, axis=-1, keepdims=True)
    xc = v - mu
    var = jnp.mean(xc * xc, axis=-1, keepdims=True)
    y = xc * lax.rsqrt(var + EPS) * lng_ref[...] + lnb_ref[...]
    o_ref[...] = (y * jax.nn.sigmoid(y)).astype(BF16)


def _conv(v, w_dw, b_dw, ln_g, ln_b, c, w_ada, b_ada):
    tm = CONV_TM
    halo_blocks = tm // CONV_HALO
    ada_tn = (N_MOD * D_MODEL - ADA_EARLY_COLS) // (SEQ // tm)
    ada_first = ADA_EARLY_COLS // ada_tn
    assert ada_tn % 128 == 0 and ADA_EARLY_COLS % ada_tn == 0
    vec = lambda n: pl.BlockSpec((1, n), lambda i: (0, 0))
    vmem = (2 * tm * CONV_WIDTH * 4 + 2 * CONV_HALO * CONV_WIDTH * 4
            + (SUBLANES * (tm + CONV_HALO) + tm) * CONV_WIDTH * 4 + 2 * tm * CONV_WIDTH * 2
            + 6 * tm * CONV_WIDTH * 4 + 2 * D_MODEL * ada_tn * 4 + D_MODEL * ada_tn * 2 + (6 << 20))
    return pl.pallas_call(
        _conv_kernel,
        grid=(SEQ // tm,),
        in_specs=[
            pl.BlockSpec((tm, CONV_WIDTH), lambda i: (i, 0)),
            pl.BlockSpec((CONV_HALO, CONV_WIDTH), lambda i: (jnp.maximum(i * halo_blocks - 1, 0), 0)),
            pl.BlockSpec((CONV_KERNEL, CONV_WIDTH), lambda i: (0, 0)),
            vec(CONV_WIDTH), vec(CONV_WIDTH), vec(CONV_WIDTH),
            vec(D_MODEL),
            pl.BlockSpec((D_MODEL, ada_tn), lambda i: (0, ada_first + i)),
            pl.BlockSpec((1, ada_tn), lambda i: (0, ada_first + i)),
        ],
        out_specs=[pl.BlockSpec((tm, CONV_WIDTH), lambda i: (i, 0)),
                   pl.BlockSpec((1, ada_tn), lambda i: (0, i))],
        out_shape=[jax.ShapeDtypeStruct((SEQ, CONV_WIDTH), BF16),
                   jax.ShapeDtypeStruct((1, N_MOD * D_MODEL - ADA_EARLY_COLS), F32)],
        scratch_shapes=[pltpu.VMEM((SUBLANES, tm + CONV_HALO, CONV_WIDTH), F32),
                        pltpu.VMEM((tm, CONV_WIDTH), F32),
                        pltpu.VMEM((CONV_KERNEL, SUBLANES, CONV_WIDTH), F32)],
        compiler_params=_params(("parallel",), vmem),
        name="convbranch",
    )(v, v, w_dw, b_dw.reshape(1, -1), ln_g.reshape(1, -1), ln_b.reshape(1, -1),
      c, w_ada, b_ada.reshape(1, -1))


N_SLOPE_PARTS = 3
AUG_LANES = 128
POS_SPLIT = 256
assert POS_SPLIT <= 256 and ATT_TQ // POS_SPLIT <= 256
assert ATT_TQ % ATT_TK == 0 and ATT_TK % ATT_CB == 0 and ATT_CB % CHUNK == 0


def _aug_operand(n_rows, parts, key_side):
    lane = lax.broadcasted_iota(jnp.int32, (n_rows, AUG_LANES), 1)
    r = lax.broadcasted_iota(jnp.int32, (n_rows, AUG_LANES), 0)
    hi = ((r // POS_SPLIT) * POS_SPLIT).astype(F32)
    lo = (r % POS_SPLIT).astype(F32)
    piece = lane % N_SLOPE_PARTS
    part = jnp.where(piece == 0, parts[0], jnp.where(piece == 1, parts[1], parts[2]))
    zero = jnp.zeros((n_rows, AUG_LANES), F32)
    if key_side:
        vals = (-part, -part, hi, lo)
    else:
        vals = (hi, lo, part, part)
    out = zero
    for g, v in enumerate(vals):
        out = jnp.where(lane // N_SLOPE_PARTS == g, v, out)
    return out


def _diagonal_blocks():
    tq, tk, cb = ATT_TQ, ATT_TK, ATT_CB
    nqc = tq // cb
    out = []
    for d in range(tq // tk):
        for b in range(2 * nqc):
            q_lo = (b % nqc) * cb
            if q_lo + cb <= d * tk:
                continue
            if q_lo >= (d + 1) * tk:
                out.append((d, b, None, tk))
            else:
                out.append((d, b, 1 + (q_lo - d * tk) // cb, min(tk, q_lo + cb - d * tk)))
    return out


def _attn_kernel(parts_ref, lam_ref, q_ref, k_ref, vt_ref, g_ref, o_ref,
                 qs_ref, kaug_ref, corr_ref, s_ref, m_ref, l_ref, acc_ref):
    tq, tk, cb = ATT_TQ, ATT_TK, ATT_CB
    nqc = tq // cb
    nb = 2 * nqc
    h = pl.program_id(0)
    i = pl.program_id(1)
    n_before = i * (tq // tk)
    parts = [parts_ref[h * N_SLOPE_PARTS + n] for n in range(N_SLOPE_PARTS)]
    sl = parts[0] + parts[1] + parts[2]

    @pl.when(i == 0)
    def _():
        kaug_ref[...] = _aug_operand(tk, parts, key_side=True).astype(BF16)
        qaug = _aug_operand(tq, parts, key_side=False).astype(BF16)
        qs_ref[0:tq, V_HEAD_DIM:] = qaug
        qs_ref[tq:, V_HEAD_DIM:] = qaug
        corr_ref[0] = jnp.zeros((tk, cb), F32)
        for c in range(tk // cb):
            krel = lax.broadcasted_iota(jnp.int32, (tk, cb), 0)
            qrel = lax.broadcasted_iota(jnp.int32, (tk, cb), 1) + c * cb
            after = jnp.maximum(krel - qrel, 0).astype(F32)
            allowed = (krel // CHUNK) <= (qrel // CHUNK)
            corr_ref[1 + c] = jnp.where(allowed, -2.0 * sl * after, NEG_INF)

    q = q_ref[...]
    lane = lax.broadcasted_iota(jnp.int32, (tq, V_HEAD_DIM), 1)
    zero = jnp.zeros_like(q)
    qs_ref[0:tq, 0:V_HEAD_DIM] = jnp.where(lane < QK_HEAD_DIM, q, zero)
    qs_ref[tq:, 0:V_HEAD_DIM] = jnp.where(lane >= QK_HEAD_DIM, q, zero)

    m_ref[...] = jnp.full_like(m_ref, NEG_INF)
    l_ref[...] = jnp.zeros_like(l_ref)
    acc_ref[...] = jnp.zeros_like(acc_ref)

    def scores(t, b, slot, n_keys=tk):
        k = k_ref[pl.ds(pl.multiple_of(t * tk, tk), n_keys), :]
        kx = jnp.concatenate([k, kaug_ref[0:n_keys, :]], axis=1)
        s_ref[slot, 0:n_keys] = lax.dot_general(kx, qs_ref[b * cb:(b + 1) * cb, :],
                                                (((1,), (1,)), ((), ())),
                                                preferred_element_type=F32)

    def value_tile(t):
        return vt_ref[:, pl.ds(pl.multiple_of(t * tk, tk), tk)]

    def softmax_values(t, b, slot, corr, vt, n_keys=tk):
        off = sl * (i * tq - t * tk).astype(F32)
        alphas, ps = [], []
        for c0 in range(0, cb, 128):
            cs = slice(c0, c0 + 128)
            s = s_ref[slot, 0:n_keys, cs]
            if corr is not None:
                s = s + corr_ref[corr, 0:n_keys, cs]
            m_prev = m_ref[b, :, cs]
            m_new = jnp.maximum(m_prev, jnp.max(s, axis=0, keepdims=True) - off)
            alpha = jnp.exp2(m_prev - m_new)
            p = jnp.exp2(s - (m_new + off))
            l_ref[b, :, cs] = alpha * l_ref[b, :, cs] + jnp.sum(p, axis=0, keepdims=True)
            m_ref[b, :, cs] = m_new
            alphas.append(alpha)
            ps.append(p.astype(BF16))
        alpha = jnp.concatenate(alphas, axis=1)
        p = jnp.concatenate(ps, axis=1)
        pv = jnp.dot(vt[:, 0:n_keys], p, preferred_element_type=F32)
        acc_ref[b] = alpha * acc_ref[b] + pv

    blocks = _diagonal_blocks()
    assert [blk[:2] for blk in blocks[:ATT_SKEW_S]] == [(0, b) for b in range(ATT_SKEW_S)]
    assert nb % ATT_RING == 0

    for b in range(ATT_SKEW_S):
        scores(0, b, b % ATT_RING)

    tiles_per_iter = tq // tk

    def body(jj, carry):
        for u in range(tiles_per_iter):
            j = jj * tiles_per_iter + u
            vt = value_tile(j)
            for b in range(nb):
                bs = b + ATT_SKEW_S
                scores(j + bs // nb, bs % nb, bs % ATT_RING)
                softmax_values(j, b, b % ATT_RING, None, vt)
        return carry

    lax.fori_loop(0, i, body, 0)

    vts = {}
    for n, (d, b, corr, n_keys) in enumerate(blocks):
        if d not in vts:
            vts[d] = value_tile(n_before + d)
        if n + ATT_SKEW_S < len(blocks):
            d2, b2, _, n_keys2 = blocks[n + ATT_SKEW_S]
            scores(n_before + d2, b2, (n + ATT_SKEW_S) % ATT_RING, n_keys2)
        softmax_values(n_before + d, b, n % ATT_RING, corr, vts[d], n_keys)

    lam = lam_ref[0]
    for c in range(nqc):
        d = acc_ref[c] / l_ref[c] - lam * (acc_ref[nqc + c] / l_ref[nqc + c])
        ms = jnp.mean(d * d, axis=0, keepdims=True)
        y = d * lax.rsqrt(ms + EPS) * g_ref[...] * (1.0 - LAMBDA_INIT)
        o_ref[c * cb:(c + 1) * cb, :] = y.T.astype(BF16)


def _slope_parts():
    rest = (2.0 ** (-8.0 * (np.arange(N_HEADS, dtype=np.float64) + 1.0) / N_HEADS) * LOG2E).astype(np.float32)
    pieces = []
    for _ in range(N_SLOPE_PARTS):
        piece = rest.astype(jnp.bfloat16).astype(np.float32)
        pieces.append(piece)
        rest = rest - piece
    return np.stack(pieces, axis=1).reshape(-1)


def _attention(qk, vt, slope_parts, lam, subln_g):
    tq, tk, cb = ATT_TQ, ATT_TK, ATT_CB
    nh = N_HEADS
    nb = 2 * tq // cb
    smem = pl.BlockSpec(memory_space=pltpu.SMEM)
    scratch = [
        pltpu.VMEM((2 * tq, V_HEAD_DIM + AUG_LANES), BF16),
        pltpu.VMEM((tk, AUG_LANES), BF16),
        pltpu.VMEM((1 + tk // cb, tk, cb), F32),
        pltpu.VMEM((ATT_RING, tk, cb), F32),
        pltpu.VMEM((nb, 1, cb), F32),
        pltpu.VMEM((nb, 1, cb), F32),
        pltpu.VMEM((nb, V_HEAD_DIM, cb), F32),
    ]
    vmem = (2 * 2 * SEQ * V_HEAD_DIM * 2 + 4 * tq * V_HEAD_DIM * 2
            + 2 * tq * (V_HEAD_DIM + AUG_LANES) * 2 + tk * AUG_LANES * 2 + (1 + tk // cb) * tk * cb * 4
            + ATT_RING * tk * cb * 4 + 2 * nb * 8 * cb * 4
            + nb * V_HEAD_DIM * cb * 4 + 8 * tk * cb * 4 + (6 << 20))
    return pl.pallas_call(
        _attn_kernel,
        grid=(nh, SEQ // tq),
        in_specs=[
            smem, smem,
            pl.BlockSpec((tq, V_HEAD_DIM), lambda h, i: (i, h)),
            pl.BlockSpec((SEQ, V_HEAD_DIM), lambda h, i: (0, nh + h)),
            pl.BlockSpec((V_HEAD_DIM, SEQ), lambda h, i: (h, 0)),
            pl.BlockSpec((V_HEAD_DIM, 1), lambda h, i: (0, 0)),
        ],
        out_specs=pl.BlockSpec((tq, V_HEAD_DIM), lambda h, i: (i, h)),
        out_shape=jax.ShapeDtypeStruct((SEQ, ATTN_WIDTH), BF16),
        scratch_shapes=scratch,
        compiler_params=_params(("parallel", "arbitrary"), vmem),
        name="diffattn",
    )(slope_parts, lam, qk, qk, vt, subln_g.reshape(V_HEAD_DIM, 1))


def _outproj_kernel(x_ref, gate_ref, gpost_ref, yc_ref, ya_ref, w_ref, o_ref, vec_ref, wbf_ref):
    @pl.when(pl.program_id(0) == 0)
    def _():
        wbf_ref[...] = w_ref[...].astype(BF16)

    y = jnp.dot(yc_ref[...], wbf_ref[0:CONV_WIDTH, :], preferred_element_type=F32)
    o_ref[...] = y + jnp.dot(ya_ref[...], wbf_ref[CONV_WIDTH:, :], preferred_element_type=F32)
    _gated_residual(x_ref, o_ref, gpost_ref, gate_ref, 1.0, o_ref, vec_ref)


def _outproj(x, mod, gate_row, g_post, y_conv, y_attn, w_out):
    tm = OUTPROJ_TM
    vmem = (2 * 2 * tm * D_MODEL * 4 + 2 * 2 * tm * CONV_WIDTH * 2 + D_MODEL * D_MODEL * (4 + 2)
            + 3 * tm * D_MODEL * 4 + (6 << 20))
    return pl.pallas_call(
        _outproj_kernel,
        grid=(SEQ // tm,),
        in_specs=[
            pl.BlockSpec((tm, D_MODEL), lambda i: (i, 0)),
            pl.BlockSpec((1, D_MODEL), lambda i: (0, gate_row)),
            pl.BlockSpec((1, D_MODEL), lambda i: (0, 0)),
            pl.BlockSpec((tm, CONV_WIDTH), lambda i: (i, 0)),
            pl.BlockSpec((tm, ATTN_WIDTH), lambda i: (i, 0)),
            pl.BlockSpec((D_MODEL, D_MODEL), lambda i: (0, 0), pipeline_mode=pl.Buffered(1)),
        ],
        out_specs=pl.BlockSpec((tm, D_MODEL), lambda i: (i, 0)),
        out_shape=jax.ShapeDtypeStruct((SEQ, D_MODEL), F32),
        scratch_shapes=[pltpu.VMEM((1, D_MODEL), F32), pltpu.VMEM((D_MODEL, D_MODEL), BF16)],
        compiler_params=_params(("arbitrary",), vmem),
        name="outproj",
    )(x, mod, g_post.reshape(1, D_MODEL), y_conv, y_attn, w_out)


def kernel(x, c, w_ada, b_ada, g_pre, g_post, w_ffn1_gate, w_ffn1_up, w_ffn1_down, w_in, b_in_conv, w_dw, b_dw, conv_ln_g, conv_ln_b, lam_q1, lam_k1, lam_q2, lam_k2, subln_g, w_out, w_ffn2_gate, w_ffn2_up, w_ffn2_down):
    x2 = x.reshape(SEQ, D_MODEL)

    mod = _ada(c, w_ada[0], b_ada[0], ADA_EARLY_COLS)

    x2 = _ffn(x2, mod, 0, g_pre[0, 0], g_post[0, 0], w_ffn1_gate[0], w_ffn1_up[0], w_ffn1_down[0])

    v_conv, qk, vt = _inproj(x2, mod, g_pre[0, 1], w_in[0], b_in_conv[0])
    y_conv, mod_late = _conv(v_conv, w_dw[0], b_dw[0], conv_ln_g[0], conv_ln_b[0], c, w_ada[0], b_ada[0])

    lam = (jnp.exp(jnp.sum(lam_q1[0].astype(F32) * lam_k1[0].astype(F32)))
           - jnp.exp(jnp.sum(lam_q2[0].astype(F32) * lam_k2[0].astype(F32)))
           + LAMBDA_INIT).reshape(1)
    slope_parts = jnp.asarray(_slope_parts())
    y_attn = _attention(qk, vt, slope_parts, lam, subln_g[0])

    x2 = _outproj(x2, mod_late, 5 - N_MOD_EARLY, g_post[0, 1], y_conv, y_attn, w_out[0])

    x2 = _ffn(x2, mod_late, 6 - N_MOD_EARLY, g_pre[0, 2], g_post[0, 2],
              w_ffn2_gate[0], w_ffn2_up[0], w_ffn2_down[0])
    return x2.reshape(1, SEQ, D_MODEL)
```
